```python
import jax, jax.numpy as jnp
from jax import lax
import numpy as np

D_MODEL = 1024
BATCH = 1
SEQ = 16384
DEPTH = 1
DEC_BATCH = 8
DEC_SEQ = 16
PAST_LEN = 2048

CHUNK = 64
N_HEADS_SB = 8
HEAD_DIM_SB = 64
WIDTH_SB = N_HEADS_SB * HEAD_DIM_SB
Q_BLOCK = 128
N_GROUPS_MLP = 8
WIDTH_MLP = WIDTH_SB
GROUP_DIM_MLP = WIDTH_MLP // N_GROUPS_MLP
MLP_CHUNK = 128
N_BRANCH = 2
IN_WIDTH = 3 * WIDTH_SB + 2 * WIDTH_MLP + N_BRANCH * D_MODEL
PEER_HEADS = 8
N_KEYS = 128
N_EXPERTS = N_KEYS * N_KEYS
PEER_D_KEY = 256
PEER_TOPK = 16
PEER_BLOCK = 128
EPS = 1e-6

kernel_name = 'hybrid_stickbreak_gmlp_peer_stream_step'


def rmsnorm(x, g):
    xf = x.astype(jnp.float32)
    r = xf * lax.rsqrt(jnp.mean(xf * xf, axis=-1, keepdims=True) + EPS)
    return (r * g.astype(jnp.float32)).astype(x.dtype)


def to_heads(t):
    B, T, _ = t.shape
    return t.reshape(B, T, N_HEADS_SB, HEAD_DIM_SB).transpose(0, 2, 1, 3)


def sb_block(q, k, v, qpos):
    z = jnp.einsum('bhtd,bhsd->bhts', q, k).astype(jnp.float32) * (HEAD_DIM_SB ** -0.5)
    kpos = jnp.arange(k.shape[2], dtype=jnp.int32)
    valid = kpos[None, :] < qpos[:, None]
    log_one_minus = jnp.where(valid, jax.nn.log_sigmoid(-z), 0.0)
    rest = lax.cumsum(log_one_minus, axis=3, reverse=True) - log_one_minus
    a = jnp.where(valid, jnp.exp(jax.nn.log_sigmoid(z) + rest), 0.0)
    return jnp.einsum('bhts,bhsd->bhtd', a.astype(v.dtype), v)


def sb_prompt(q, k, v):
    B, H, S, dh = q.shape
    nb = S // Q_BLOCK
    qb = q.reshape(B, H, nb, Q_BLOCK, dh).transpose(2, 0, 1, 3, 4)
    pos = jnp.arange(S, dtype=jnp.int32).reshape(nb, Q_BLOCK)
    ob = lax.map(lambda a: sb_block(a[0], k, v, a[1]), (qb, pos))
    return ob.transpose(1, 2, 0, 3, 4).reshape(B, H, S, dh)


def sb_sample(q, k_new, v_new, k_past, v_past):
    k_all = jnp.concatenate([k_past, k_new], axis=2)
    v_all = jnp.concatenate([v_past, v_new], axis=2)
    qpos = k_past.shape[2] + jnp.arange(q.shape[2], dtype=jnp.int32)
    return sb_block(q, k_all, v_all, qpos)


def spatial_mix(u, v, w_s, b_s):
    B, T, G, dg = v.shape
    n = min(T, MLP_CHUNK)
    i = jnp.arange(n)
    mask = (i[None, :] // CHUNK) <= (i[:, None] // CHUNK)
    w = jnp.where(mask[None], w_s[:, :n, :n], 0.0).astype(v.dtype)
    vc = v.reshape(B, T // n, n, G, dg)
    mixed = jnp.einsum('gij,bcjgd->bcigd', w, vc) + b_s[:, :n].T[None, None, :, :, None]
    return u * mixed.reshape(B, T, G, dg)


def mixer_sublayer(x, k_past, v_past, w_in, b_gate, gmlp_norm_g, w_s, b_s, w_branch, w_out, norm_g):
    B, T, _ = x.shape
    h = rmsnorm(x, norm_g)
    p = h @ w_in
    splits = [WIDTH_SB, 2 * WIDTH_SB, 3 * WIDTH_SB, 3 * WIDTH_SB + WIDTH_MLP, 3 * WIDTH_SB + 2 * WIDTH_MLP]
    q, k, v, u_m, v_m, g = jnp.split(p, splits, axis=-1)
    q, k, v = to_heads(q), to_heads(k), to_heads(v)
    if k_past is None:
        o_a = sb_prompt(q, k, v)
    else:
        o_a = sb_sample(q, k, v, k_past, v_past)
    o_a = o_a.transpose(0, 2, 1, 3).reshape(B, T, WIDTH_SB)
    u_m = jax.nn.gelu(u_m).reshape(B, T, N_GROUPS_MLP, GROUP_DIM_MLP)
    v_m = rmsnorm(jax.nn.gelu(v_m), gmlp_norm_g)
    o_b = spatial_mix(u_m, v_m.reshape(B, T, N_GROUPS_MLP, GROUP_DIM_MLP), w_s, b_s).reshape(B, T, WIDTH_MLP)
    gates = jax.nn.sigmoid(g + b_gate).reshape(B, T, N_BRANCH, D_MODEL)
    branches = jnp.einsum('btnw,nwd->btnd', jnp.stack([o_a, o_b], axis=2), w_branch)
    y = x + jnp.sum(gates * branches, axis=2) @ w_out
    return y, k, v, v_m


def peer_block(hb, w_query, sub_keys, expert_u, expert_v):
    TB = hb.shape[0]
    q = (hb @ w_query).reshape(TB, PEER_HEADS, 2, PEER_D_KEY // 2)
    s = jnp.einsum('thcd,hcnd->thcn', q, sub_keys).astype(jnp.float32)
    s1, i1 = lax.top_k(s[:, :, 0], PEER_TOPK)
    s2, i2 = lax.top_k(s[:, :, 1], PEER_TOPK)
    cand = (s1[..., :, None] + s2[..., None, :]).reshape(TB, PEER_HEADS, PEER_TOPK * PEER_TOPK)
    sc, ci = lax.top_k(cand, PEER_TOPK)
    e = (jnp.take_along_axis(i1, ci // PEER_TOPK, axis=-1) * N_KEYS
         + jnp.take_along_axis(i2, ci % PEER_TOPK, axis=-1))
    gate = jax.nn.softmax(sc, axis=-1)
    act = jax.nn.gelu(jnp.einsum('thkd,td->thk', expert_u[e], hb))
    wgt = (gate * act.astype(jnp.float32)).astype(hb.dtype)
    return jnp.einsum('thk,thkd->td', wgt, expert_v[e])


def ffn_sublayer(x, norm_g, w_query, sub_keys, expert_u, expert_v):
    B, T, D = x.shape
    h = rmsnorm(x, norm_g).reshape(B * T, D)
    n = B * T
    pad = (-n) % PEER_BLOCK
    hp = jnp.pad(h, ((0, pad), (0, 0))).reshape(-1, PEER_BLOCK, D)
    out = lax.map(lambda hb: peer_block(hb, w_query, sub_keys, expert_u, expert_v), hp)
    return x + out.reshape(-1, D)[:n].reshape(B, T, D)


def setup_inputs(seed: int = 0) -> dict:
    key = jax.random.key(seed)
    ks = jax.random.split(key, 18)

    def nrm(k, shape, s):
        return jax.random.normal(k, shape, jnp.float32) * s

    return {
        'x_prompt': nrm(ks[0], (BATCH, SEQ, D_MODEL), 1.0),
        'x_sample': nrm(ks[1], (DEC_BATCH, DEC_SEQ, D_MODEL), 1.0),
        'cache_k': nrm(ks[2], (DEPTH, DEC_BATCH, N_HEADS_SB, PAST_LEN, HEAD_DIM_SB), 1.0),
        'cache_v': nrm(ks[3], (DEPTH, DEC_BATCH, N_HEADS_SB, PAST_LEN, HEAD_DIM_SB), 1.0),
        'norm_mix_g': 1.0 + nrm(ks[4], (DEPTH, D_MODEL), 0.02),
        'w_in': nrm(ks[5], (DEPTH, D_MODEL, IN_WIDTH), D_MODEL ** -0.5),
        'b_gate': nrm(ks[6], (DEPTH, N_BRANCH * D_MODEL), 0.02),
        'gmlp_norm_g': 1.0 + nrm(ks[7], (DEPTH, WIDTH_MLP), 0.02),
        'w_s': nrm(ks[8], (DEPTH, N_GROUPS_MLP, MLP_CHUNK, MLP_CHUNK), MLP_CHUNK ** -0.5),
        'b_s': 1.0 + nrm(ks[9], (DEPTH, N_GROUPS_MLP, MLP_CHUNK), 0.02),
        'w_branch': nrm(ks[10], (DEPTH, N_BRANCH, WIDTH_SB, D_MODEL), WIDTH_SB ** -0.5),
        'w_out': nrm(ks[11], (DEPTH, D_MODEL, D_MODEL), D_MODEL ** -0.5),
        'norm_ffn_g': 1.0 + nrm(ks[12], (DEPTH, D_MODEL), 0.02),
        'w_query': nrm(ks[13], (DEPTH, D_MODEL, PEER_HEADS * PEER_D_KEY), D_MODEL ** -0.5),
        'sub_keys': nrm(ks[14], (DEPTH, PEER_HEADS, 2, N_KEYS, PEER_D_KEY // 2), (PEER_D_KEY // 2) ** -0.5),
        'expert_u': nrm(ks[15], (DEPTH, N_EXPERTS, D_MODEL), D_MODEL ** -0.5),
        'expert_v': nrm(ks[16], (DEPTH, N_EXPERTS, D_MODEL), PEER_HEADS ** -0.5),
        'norm_final_g': 1.0 + nrm(ks[17], (D_MODEL,), 0.02),
    }


def reference(x_prompt, x_sample, cache_k, cache_v, norm_mix_g, w_in, b_gate, gmlp_norm_g, w_s, b_s,
              w_branch, w_out, norm_ffn_g, w_query, sub_keys, expert_u, expert_v, norm_final_g):
    xp, xs = x_prompt, x_sample
    kp_l, vp_l, ks_l, vs_l, gv_l = [], [], [], [], []
    for l in range(DEPTH):
        mix = (w_in[l], b_gate[l], gmlp_norm_g[l], w_s[l], b_s[l], w_branch[l], w_out[l], norm_mix_g[l])
        xp, kp, vp, _ = mixer_sublayer(xp, None, None, *mix)
        xs, k_s, v_s, gv_s = mixer_sublayer(xs, cache_k[l], cache_v[l], *mix)
        ffn = (norm_ffn_g[l], w_query[l], sub_keys[l], expert_u[l], expert_v[l])
        xp = ffn_sublayer(xp, *ffn)
        xs = ffn_sublayer(xs, *ffn)
        kp_l.append(kp)
        vp_l.append(vp)
        ks_l.append(k_s)
        vs_l.append(v_s)
        gv_l.append(gv_s)
    y_prompt = rmsnorm(xp, norm_final_g)
    y_sample = rmsnorm(xs, norm_final_g)
    return (y_prompt, y_sample, jnp.stack(kp_l), jnp.stack(vp_l), jnp.stack(ks_l), jnp.stack(vs_l), jnp.stack(gv_l))
```

```python
import functools

import jax
import jax.numpy as jnp
from jax import lax
from jax.experimental import pallas as pl
from jax.experimental.pallas import tpu as pltpu

D_MODEL = 1024
N_HEADS_SB = 8
HEAD_DIM_SB = 64
WIDTH = N_HEADS_SB * HEAD_DIM_SB
N_GROUPS_MLP = 8
GROUP_DIM_MLP = WIDTH // N_GROUPS_MLP
MLP_CHUNK = 128
CHUNK = 64
PEER_HEADS = 8
N_KEYS = 128
N_EXPERTS = N_KEYS * N_KEYS
PEER_TOPK = 16
EPS = 1e-6
SB_SCALE = HEAD_DIM_SB ** -0.5

LANES = 128
KEY_BLOCK = LANES
Q_BLOCK = LANES
SB_DEAD_LOGIT = -120.0
VMEM_LIMIT_BYTES = 56 * 1024 * 1024

_BF = jnp.bfloat16
_F32 = jnp.float32
_NEG_INF = float("-inf")


def _rms(x, g):
    return x * lax.rsqrt(jnp.mean(x * x, axis=-1, keepdims=True) + EPS) * g


def _gelu(x):
    return 0.5 * x * (1.0 + jnp.tanh(0.7978845608028654 * (x + 0.044715 * (x * x * x))))


def _dot(a, b):
    return jnp.dot(a, b, preferred_element_type=_F32)


def _params(*sem):
    return pltpu.CompilerParams(dimension_semantics=sem, vmem_limit_bytes=VMEM_LIMIT_BYTES)


def _qkv_body(x_ref, g_ref, w_ref, q_ref, k_ref, v_ref, kb_ref, vb_ref):
    h = _rms(x_ref[...], g_ref[...]).astype(_BF)
    p = _dot(h, w_ref[...])
    q_ref[...] = (p[:, :WIDTH] * SB_SCALE).astype(_BF)
    k = p[:, WIDTH:2 * WIDTH]
    v = p[:, 2 * WIDTH:]
    k_ref[...] = k
    v_ref[...] = v
    kb_ref[...] = k.astype(_BF)
    vb_ref[...] = v.astype(_BF)


def _qkv(x, g, w_qkv, tt):
    t = x.shape[0]
    row = lambda i: (i, 0)
    fixed = lambda i: (0, 0)
    return pl.pallas_call(
        _qkv_body,
        grid=(t // tt,),
        in_specs=[pl.BlockSpec((tt, D_MODEL), row),
                  pl.BlockSpec((1, D_MODEL), fixed),
                  pl.BlockSpec((D_MODEL, 3 * WIDTH), fixed)],
        out_specs=[pl.BlockSpec((tt, WIDTH), row)] * 5,
        out_shape=[jax.ShapeDtypeStruct((t, WIDTH), _BF),
                   jax.ShapeDtypeStruct((t, WIDTH), _F32),
                   jax.ShapeDtypeStruct((t, WIDTH), _F32),
                   jax.ShapeDtypeStruct((t, WIDTH), _BF),
                   jax.ShapeDtypeStruct((t, WIDTH), _BF)],
        compiler_params=_params("parallel"),
        name="qkv_proj",
    )(x, g, w_qkv)


def _sb_block(q, kb, vb, valid, r, uo):
    z = lax.dot_general(q, kb, (((1,), (1,)), ((), ())), preferred_element_type=_F32)
    softplus = jnp.maximum(z, 0.0) + jnp.log1p(jnp.exp(-jnp.abs(z)))
    lom = jnp.where(valid, -softplus, 0.0)
    hi = lom.astype(_BF)
    lo = (lom - hi.astype(_F32)).astype(_BF)
    cs = _dot(hi, uo) + _dot(lo, uo)
    a = jnp.where(valid, jnp.exp(z + cs[:, :KEY_BLOCK] + r), 0.0)
    return _dot(a.astype(_BF), vb), r + cs[:, KEY_BLOCK:]


def _sb_prompt_body(q_ref, k_ref, v_ref, uo_ref, o_ref, acc_ref, r_ref):
    i = pl.program_id(1)
    q = q_ref[...]
    lane = lax.broadcasted_iota(jnp.int32, (Q_BLOCK, LANES), 1)
    zero = jnp.zeros_like(q)
    q2 = jnp.concatenate([jnp.where(lane < HEAD_DIM_SB, q, zero),
                          jnp.where(lane >= HEAD_DIM_SB, q, zero)], axis=0)
    acc_ref[...] = jnp.zeros_like(acc_ref)
    r_ref[...] = jnp.zeros_like(r_ref)
    row = lax.broadcasted_iota(jnp.int32, (2 * Q_BLOCK, KEY_BLOCK), 0) & (Q_BLOCK - 1)
    col = lax.broadcasted_iota(jnp.int32, (2 * Q_BLOCK, KEY_BLOCK), 1)
    qpos = row + i * Q_BLOCK

    def cond(c):
        j, rmax = c
        return jnp.logical_and(j >= 0, rmax > SB_DEAD_LOGIT)

    def body(c):
        j, _ = c
        off = pl.multiple_of(j * KEY_BLOCK, KEY_BLOCK)
        valid = (col + j * KEY_BLOCK) < qpos
        pv, rn = _sb_block(q2, k_ref[pl.ds(off, KEY_BLOCK), :], v_ref[pl.ds(off, KEY_BLOCK), :],
                           valid, r_ref[...], uo_ref[...])
        acc_ref[...] += pv
        r_ref[...] = rn
        return j - 1, jnp.max(rn)

    lax.while_loop(cond, body, (i, jnp.float32(0.0)))
    acc = acc_ref[...]
    o_ref[...] = jnp.where(lane < HEAD_DIM_SB, acc[:Q_BLOCK], acc[Q_BLOCK:]).astype(o_ref.dtype)


def _sb_prompt(q, kb, vb, uo):
    s = q.shape[0]
    return pl.pallas_call(
        _sb_prompt_body,
        grid=(WIDTH // LANES, s // Q_BLOCK),
        in_specs=[pl.BlockSpec((Q_BLOCK, LANES), lambda p, i: (i, p)),
                  pl.BlockSpec((s, LANES), lambda p, i: (0, p)),
                  pl.BlockSpec((s, LANES), lambda p, i: (0, p)),
                  pl.BlockSpec((KEY_BLOCK, 2 * KEY_BLOCK), lambda p, i: (0, 0))],
        out_specs=pl.BlockSpec((Q_BLOCK, LANES), lambda p, i: (i, p)),
        out_shape=jax.ShapeDtypeStruct((s, WIDTH), _BF),
        scratch_shapes=[pltpu.VMEM((2 * Q_BLOCK, LANES), _F32),
                        pltpu.VMEM((2 * Q_BLOCK, LANES), _F32)],
        compiler_params=_params("parallel", "parallel"),
        name="sb_prompt",
    )(q, kb, vb, uo)


def _sb_sample_body(q_ref, kn_ref, vn_ref, kp_ref, vp_ref, uo_ref, o_ref, acc_ref, r_ref):
    q = q_ref[...]
    tq = q.shape[0]
    row = lax.broadcasted_iota(jnp.int32, (tq, KEY_BLOCK), 0)
    col = lax.broadcasted_iota(jnp.int32, (tq, KEY_BLOCK), 1)
    pv, r0 = _sb_block(q, kn_ref[...], vn_ref[...], col < row,
                       jnp.zeros((tq, LANES), _F32), uo_ref[...])
    acc_ref[...] = pv
    r_ref[...] = r0
    always = col >= 0
    n_past = kp_ref.shape[0] // KEY_BLOCK

    def cond(c):
        j, rmax = c
        return jnp.logical_and(j >= 0, rmax > SB_DEAD_LOGIT)

    def body(c):
        j, _ = c
        off = pl.multiple_of(j * KEY_BLOCK, KEY_BLOCK)
        pv, rn = _sb_block(q, kp_ref[pl.ds(off, KEY_BLOCK), :].astype(_BF),
                           vp_ref[pl.ds(off, KEY_BLOCK), :].astype(_BF),
                           always, r_ref[...], uo_ref[...])
        acc_ref[...] += pv
        r_ref[...] = rn
        return j - 1, jnp.max(rn)

    lax.while_loop(cond, body, (jnp.int32(n_past - 1), jnp.max(r0)))
    o_ref[...] = acc_ref[...]


def _sb_sample(q, kn, vn, kp, vp, uo):
    b, h, tq, dh = q.shape
    past = kp.shape[2]
    blk = lambda n: pl.BlockSpec((None, None, n, dh), lambda bi, hi: (bi, hi, 0, 0))
    return pl.pallas_call(
        _sb_sample_body,
        grid=(b, h),
        in_specs=[blk(tq), blk(KEY_BLOCK), blk(KEY_BLOCK), blk(past), blk(past),
                  pl.BlockSpec((KEY_BLOCK, 2 * KEY_BLOCK), lambda bi, hi: (0, 0))],
        out_specs=blk(tq),
        out_shape=jax.ShapeDtypeStruct((b, h, tq, dh), _F32),
        scratch_shapes=[pltpu.VMEM((tq, dh), _F32), pltpu.VMEM((tq, LANES), _F32)],
        compiler_params=_params("parallel", "parallel"),
        name="sb_sample",
    )(q, kn, vn, kp, vp, uo)


def _mix_body(x_ref, oa_ref, g1_ref, w3_ref, bg_ref, gg_ref, wm_ref, bm_ref, wb_ref, wo_ref,
              g2_ref, y1_ref, h2t_ref, *gv_ref):
    x = x_ref[...]
    tt = x.shape[0]
    h = _rms(x, g1_ref[...]).astype(_BF)
    p = _dot(h, w3_ref[...])
    u = _gelu(p[:, :WIDTH])
    vn = _rms(_gelu(p[:, WIDTH:2 * WIDTH]), gg_ref[...])
    if gv_ref:
        gv_ref[0][...] = vn
    vnb = vn.astype(_BF)
    lane = lax.broadcasted_iota(jnp.int32, (MLP_CHUNK, LANES), 1)
    bias = bm_ref[...]
    chunks = []
    for c in range(tt // MLP_CHUNK):
        cols = []
        for m in range(WIDTH // LANES):
            vc = vnb[c * MLP_CHUNK:(c + 1) * MLP_CHUNK, m * LANES:(m + 1) * LANES]
            cols.append(jnp.where(lane < GROUP_DIM_MLP, _dot(wm_ref[2 * m], vc),
                                  _dot(wm_ref[2 * m + 1], vc)))
        chunks.append(jnp.concatenate(cols, axis=1) + bias)
    mixed = chunks[0] if len(chunks) == 1 else jnp.concatenate(chunks, axis=0)
    ob = (u * mixed).astype(_BF)
    gates = jax.nn.sigmoid(p[:, 2 * WIDTH:] + bg_ref[...])
    m = gates[:, :D_MODEL] * _dot(oa_ref[...], wb_ref[0]) + gates[:, D_MODEL:] * _dot(ob, wb_ref[1])
    y1 = x + _dot(m.astype(_BF), wo_ref[...])
    y1_ref[...] = y1
    h2t_ref[...] = _rms(y1, g2_ref[...]).T.astype(_BF)


def _mix(x, oa, g1, w3, bg, gg, wm, bm, wb, wo, g2, tt, emit_gv):
    t = x.shape[0]
    row = lambda i: (i, 0)
    fixed2 = lambda i: (0, 0)
    fixed3 = lambda i: (0, 0, 0)
    out_specs = [pl.BlockSpec((tt, D_MODEL), row), pl.BlockSpec((D_MODEL, tt), lambda i: (0, i))]
    out_shape = [jax.ShapeDtypeStruct((t, D_MODEL), _F32), jax.ShapeDtypeStruct((D_MODEL, t), _BF)]
    if emit_gv:
        out_specs.append(pl.BlockSpec((tt, WIDTH), row))
        out_shape.append(jax.ShapeDtypeStruct((t, WIDTH), _F32))
    return pl.pallas_call(
        _mix_body,
        grid=(t // tt,),
        in_specs=[pl.BlockSpec((tt, D_MODEL), row),
                  pl.BlockSpec((tt, WIDTH), row),
                  pl.BlockSpec((1, D_MODEL), fixed2),
                  pl.BlockSpec((D_MODEL, 2 * WIDTH + 2 * D_MODEL), fixed2),
                  pl.BlockSpec((1, 2 * D_MODEL), fixed2),
                  pl.BlockSpec((1, WIDTH), fixed2),
                  pl.BlockSpec((N_GROUPS_MLP, MLP_CHUNK, MLP_CHUNK), fixed3),
                  pl.BlockSpec((MLP_CHUNK, WIDTH), fixed2),
                  pl.BlockSpec((2, WIDTH, D_MODEL), fixed3),
                  pl.BlockSpec((D_MODEL, D_MODEL), fixed2),
                  pl.BlockSpec((1, D_MODEL), fixed2)],
        out_specs=out_specs,
        out_shape=out_shape,
        compiler_params=_params("parallel"),
        name="mixer_out",
    )(x, oa, g1, w3, bg, gg, wm, bm, wb, wo, g2)


def _top16(s):
    n = s.shape[0]
    row = lax.broadcasted_iota(jnp.int32, s.shape, 0)
    rank = jnp.full(s.shape, N_KEYS - 1, jnp.int32)
    vals = []
    x = s
    for a in range(PEER_TOPK):
        m = jnp.max(x, axis=0, keepdims=True)
        idx = jnp.min(jnp.where(x == m, row, n), axis=0, keepdims=True)
        hit = row == idx
        x = jnp.where(hit, _NEG_INF, x)
        rank = jnp.where(hit, a, rank)
        vals.append(m)
    return jnp.concatenate(vals, axis=0), rank


def _staircase(v1, v2):
    arow = lax.broadcasted_iota(jnp.int32, v1.shape, 0)
    count = jnp.zeros(v1.shape, jnp.int32)
    for _ in range(PEER_TOPK):
        nxt = jnp.full(v1.shape, _NEG_INF, _F32)
        for b in range(PEER_TOPK):
            nxt = jnp.where(count == b, v2[b:b + 1, :], nxt)
        cand = v1 + nxt
        m = jnp.max(cand, axis=0, keepdims=True)
        asel = jnp.min(jnp.where(cand == m, arow, PEER_TOPK), axis=0, keepdims=True)
        count = count + (arow == asel).astype(jnp.int32)
    return count


def _peer_body(h2t_ref, y1_ref, wqt_ref, sk_ref, u_ref, vt_ref, gf_ref, y_ref,
               sc_ref, lr_ref, f1_ref, r2_ref, e2_ref, wt_ref, acc_ref):
    c = pl.program_id(1)
    tt = h2t_ref.shape[1]
    nlg = tt // LANES
    ec = u_ref.shape[0]
    n1_per_chunk = ec // N_KEYS
    h2t = h2t_ref[...]

    @pl.when(c == 0)
    def _route():
        acc_ref[...] = jnp.zeros_like(acc_ref)
        kd = 2 * N_KEYS
        for h in range(PEER_HEADS):
            qt = _dot(wqt_ref[h * kd:(h + 1) * kd, :], h2t).astype(_BF)
            for half in range(2):
                st = _dot(sk_ref[h, half], qt[half * N_KEYS:(half + 1) * N_KEYS, :])
                for lg in range(nlg):
                    sc_ref[h, half, lg] = st[:, lg * LANES:(lg + 1) * LANES]

        def route(idx, carry):
            h = idx // nlg
            lg = idx % nlg
            s1 = sc_ref[h, 0, lg]
            s2 = sc_ref[h, 1, lg]
            v1, rank1 = _top16(s1)
            v2, rank2 = _top16(s2)
            stair = _staircase(v1, v2)
            e1s = jnp.exp(v1 - v1[0:1, :])
            e2s = jnp.exp(v2 - v2[0:1, :])
            pref = jnp.zeros_like(e2s)
            for b in range(PEER_TOPK):
                pref = pref + jnp.where(stair > b, e2s[b:b + 1, :], 0.0)
            zsum = jnp.sum(e1s * pref, axis=0, keepdims=True)
            lr = jnp.zeros(s1.shape, jnp.int32)
            for a in range(PEER_TOPK):
                lr = jnp.where(rank1 == a, stair[a:a + 1, :], lr)
            lr_ref[h, lg] = lr.astype(_F32)
            f1_ref[h, lg] = jnp.exp(s1 - v1[0:1, :]) / zsum
            r2_ref[h, lg] = rank2.astype(_F32)
            e2_ref[h, lg] = jnp.exp(s2 - v2[0:1, :])
            return carry

        lax.fori_loop(0, PEER_HEADS * nlg, route, 0)

    st = _dot(u_ref[...], h2t)
    for k in range(n1_per_chunk):
        n1 = c * n1_per_chunk + k
        for lg in range(nlg):
            gate = jnp.zeros((N_KEYS, LANES), _F32)
            for h in range(PEER_HEADS):
                lrow = lr_ref[h, lg, pl.ds(n1, 1), :]
                frow = f1_ref[h, lg, pl.ds(n1, 1), :]
                gate = gate + jnp.where(r2_ref[h, lg] < lrow, e2_ref[h, lg] * frow, 0.0)
            act = _gelu(st[k * N_KEYS:(k + 1) * N_KEYS, lg * LANES:(lg + 1) * LANES])
            wt_ref[k * N_KEYS:(k + 1) * N_KEYS, lg * LANES:(lg + 1) * LANES] = (gate * act).astype(_BF)
    acc_ref[...] += _dot(vt_ref[...], wt_ref[...])

    @pl.when(c == pl.num_programs(1) - 1)
    def _finish():
        y_ref[...] = _rms(y1_ref[...] + acc_ref[...].T, gf_ref[...])


def _peer(h2t, y1, wqt, sk, u, vt, gf, tt, ec):
    t = y1.shape[0]
    nlg = tt // LANES
    tile = lambda i, c: (i, 0)
    slab = lambda: pltpu.VMEM((PEER_HEADS, nlg, N_KEYS, LANES), _F32)
    return pl.pallas_call(
        _peer_body,
        grid=(t // tt, N_EXPERTS // ec),
        in_specs=[pl.BlockSpec((D_MODEL, tt), lambda i, c: (0, i)),
                  pl.BlockSpec((tt, D_MODEL), tile),
                  pl.BlockSpec((PEER_HEADS * 2 * N_KEYS, D_MODEL), lambda i, c: (0, 0)),
                  pl.BlockSpec((PEER_HEADS, 2, N_KEYS, N_KEYS), lambda i, c: (0, 0, 0, 0)),
                  pl.BlockSpec((ec, D_MODEL), lambda i, c: (c, 0)),
                  pl.BlockSpec((D_MODEL, ec), lambda i, c: (0, c)),
                  pl.BlockSpec((1, D_MODEL), lambda i, c: (0, 0))],
        out_specs=pl.BlockSpec((tt, D_MODEL), tile),
        out_shape=jax.ShapeDtypeStruct((t, D_MODEL), _F32),
        scratch_shapes=[pltpu.VMEM((PEER_HEADS, 2, nlg, N_KEYS, LANES), _F32),
                        slab(), slab(), slab(), slab(),
                        pltpu.VMEM((ec, tt), _BF),
                        pltpu.VMEM((D_MODEL, tt), _F32)],
        compiler_params=_params("parallel", "arbitrary"),
        name="peer_ffn",
    )(h2t, y1, wqt, sk, u, vt, gf)


def _to_heads(t2d, b):
    t = t2d.shape[0] // b
    return t2d.reshape(b, t, N_HEADS_SB, HEAD_DIM_SB).transpose(0, 2, 1, 3)


def kernel(x_prompt, x_sample, cache_k, cache_v, norm_mix_g, w_in, b_gate, gmlp_norm_g, w_s, b_s,
           w_branch, w_out, norm_ffn_g, w_query, sub_keys, expert_u, expert_v, norm_final_g):
    assert w_in.shape[0] == 1, "single trunk layer"
    bp, sp, _ = x_prompt.shape
    bs, ss, _ = x_sample.shape
    assert bp == 1 and sp % MLP_CHUNK == 0 and bs * ss == LANES and ss <= CHUNK

    w_in_b = w_in[0].astype(_BF)
    w_qkv = w_in_b[:, :3 * WIDTH]
    w_rest = w_in_b[:, 3 * WIDTH:]
    g_mix = norm_mix_g[0][None, :]
    g_ffn = norm_ffn_g[0][None, :]
    g_fin = norm_final_g[None, :]
    g_mlp = gmlp_norm_g[0][None, :]
    bg = b_gate[0][None, :]
    wb = w_branch[0].astype(_BF)
    wo = w_out[0].astype(_BF)
    wqt = w_query[0].T.astype(_BF)
    sk = sub_keys[0].astype(_BF)
    u_b = expert_u[0].astype(_BF)
    vt_b = expert_v[0].T.astype(_BF)
    jj = lax.broadcasted_iota(jnp.int32, (KEY_BLOCK, 2 * KEY_BLOCK), 0)
    cc = lax.broadcasted_iota(jnp.int32, (KEY_BLOCK, 2 * KEY_BLOCK), 1)
    uo = ((cc >= KEY_BLOCK) | (jj >= cc)).astype(_BF)

    pos = jnp.arange(MLP_CHUNK)
    causal = (pos[None, :] // CHUNK) <= (pos[:, None] // CHUNK)
    wm_p = jnp.where(causal[None], w_s[0], 0.0).astype(_BF)
    bm_p = jnp.repeat(b_s[0].T, GROUP_DIM_MLP, axis=1)
    same_stream = (pos[None, :] // ss) == (pos[:, None] // ss)
    wm_s = jnp.where(same_stream[None], jnp.tile(w_s[0][:, :ss, :ss], (1, bs, bs)), 0.0).astype(_BF)
    bm_s = jnp.repeat(jnp.tile(b_s[0][:, :ss], (1, bs)).T, GROUP_DIM_MLP, axis=1)

    xp = x_prompt[0]
    q_p, k_p, v_p, kb_p, vb_p = _qkv(xp, g_mix, w_qkv, 512)
    oa_p = _sb_prompt(q_p, kb_p, vb_p, uo)
    y1_p, h2t_p = _mix(xp, oa_p, g_mix, w_rest, bg, g_mlp, wm_p, bm_p, wb, wo, g_ffn, 512, False)
    y_p = _peer(h2t_p, y1_p, wqt, sk, u_b, vt_b, g_fin, 512, 512)

    xs = x_sample.reshape(bs * ss, D_MODEL)
    q_s, k_s, v_s, kb_s, vb_s = _qkv(xs, g_mix, w_qkv, LANES)
    pad = ((0, 0), (0, 0), (0, KEY_BLOCK - ss), (0, 0))
    oa_s = _sb_sample(_to_heads(q_s, bs), jnp.pad(_to_heads(kb_s, bs), pad),
                      jnp.pad(_to_heads(vb_s, bs), pad), cache_k[0], cache_v[0], uo)
    oa_s = oa_s.transpose(0, 2, 1, 3).reshape(bs * ss, WIDTH).astype(_BF)
    y1_s, h2t_s, gv_s = _mix(xs, oa_s, g_mix, w_rest, bg, g_mlp, wm_s, bm_s, wb, wo, g_ffn,
                             LANES, True)
    y_s = _peer(h2t_s, y1_s, wqt, sk, u_b, vt_b, g_fin, LANES, 512)

    return (y_p[None],
            y_s.reshape(bs, ss, D_MODEL),
            _to_heads(k_p, bp)[None],
            _to_heads(v_p, bp)[None],
            _to_heads(k_s, bs)[None],
            _to_heads(v_s, bs)[None],
            gv_s.reshape(bs, ss, WIDTH)[None])
```

```python
import functools

import jax
import jax.numpy as jnp
from jax import lax
from jax.experimental import pallas as pl
from jax.experimental.pallas import tpu as pltpu

D_MODEL = 1024
N_HEADS_SB = 8
HEAD_DIM_SB = 64
WIDTH = N_HEADS_SB * HEAD_DIM_SB
N_GROUPS_MLP = 8
GROUP_DIM_MLP = WIDTH // N_GROUPS_MLP
MLP_CHUNK = 128
CHUNK = 64
PEER_HEADS = 8
N_KEYS = 128
N_EXPERTS = N_KEYS * N_KEYS
PEER_TOPK = 16
EPS = 1e-6
SB_SCALE = HEAD_DIM_SB ** -0.5

LANES = 128
KEY_BLOCK = LANES
Q_BLOCK = LANES
SB_DEAD_LOGIT = -120.0
VMEM_LIMIT_BYTES = 56 * 1024 * 1024
PEER_SUB_EXPERTS = 512

_BF = jnp.bfloat16
_F32 = jnp.float32
_NEG_INF = float("-inf")


def _rms(x, g):
    return x * lax.rsqrt(jnp.mean(x * x, axis=-1, keepdims=True) + EPS) * g


def _gelu(x):
    c = 0.7978845608028654
    half = 0.5 * x
    return half + half * jnp.tanh(x * (c + (c * 0.044715) * (x * x)))


def _dot(a, b):
    return jnp.dot(a, b, preferred_element_type=_F32)


def _params(*sem):
    return pltpu.CompilerParams(dimension_semantics=sem, vmem_limit_bytes=VMEM_LIMIT_BYTES)


def _qkv_body(x_ref, g_ref, w_ref, q_ref, k_ref, v_ref, kb_ref, vb_ref):
    h = _rms(x_ref[...], g_ref[...]).astype(_BF)
    p = _dot(h, w_ref[...])
    q_ref[...] = (p[:, :WIDTH] * SB_SCALE).astype(_BF)
    k = p[:, WIDTH:2 * WIDTH]
    v = p[:, 2 * WIDTH:]
    k_ref[...] = k
    v_ref[...] = v
    kb_ref[...] = k.astype(_BF)
    vb_ref[...] = v.astype(_BF)


def _qkv(x, g, w_qkv, tt):
    t = x.shape[0]
    row = lambda i: (i, 0)
    fixed = lambda i: (0, 0)
    return pl.pallas_call(
        _qkv_body,
        grid=(t // tt,),
        in_specs=[pl.BlockSpec((tt, D_MODEL), row),
                  pl.BlockSpec((1, D_MODEL), fixed),
                  pl.BlockSpec((D_MODEL, 3 * WIDTH), fixed)],
        out_specs=[pl.BlockSpec((tt, WIDTH), row)] * 5,
        out_shape=[jax.ShapeDtypeStruct((t, WIDTH), _BF),
                   jax.ShapeDtypeStruct((t, WIDTH), _F32),
                   jax.ShapeDtypeStruct((t, WIDTH), _F32),
                   jax.ShapeDtypeStruct((t, WIDTH), _BF),
                   jax.ShapeDtypeStruct((t, WIDTH), _BF)],
        compiler_params=_params("parallel"),
        name="qkv_proj",
    )(x, g, w_qkv)


def _sb_blocks(qs, kbs, vbs, valid, rs, uo):
    zs = [lax.dot_general(q, kb, (((1,), (1,)), ((), ())), preferred_element_type=_F32)
          for q, kb in zip(qs, kbs)]
    hls = []
    for z in zs:
        softplus = jnp.maximum(z, 0.0) + jnp.log1p(jnp.exp(-jnp.abs(z)))
        lom = jnp.where(valid, -softplus, 0.0)
        hi = lom.astype(_BF)
        lo = (lom - hi.astype(_F32)).astype(_BF)
        hls.append(jnp.concatenate([hi, lo], axis=1))
    css = [_dot(hl, uo) for hl in hls]
    ws = [jnp.where(valid, jnp.exp(z + cs[:, :KEY_BLOCK] + r), 0.0).astype(_BF)
          for z, cs, r in zip(zs, css, rs)]
    pvs = [_dot(w, vb) for w, vb in zip(ws, vbs)]
    return pvs, [r + cs[:, KEY_BLOCK:] for r, cs in zip(rs, css)]


def _sb_prompt_body(q_ref, k_ref, v_ref, uo_ref, o_ref, q2_ref, acc_ref, r_ref):
    i = pl.program_id(0)
    n_pair = WIDTH // LANES
    lane = lax.broadcasted_iota(jnp.int32, (Q_BLOCK, LANES), 1)
    for p in range(n_pair):
        q = q_ref[:, p * LANES:(p + 1) * LANES]
        zero = jnp.zeros_like(q)
        q2_ref[p] = jnp.concatenate([jnp.where(lane < HEAD_DIM_SB, q, zero),
                                     jnp.where(lane >= HEAD_DIM_SB, q, zero)], axis=0)
    acc_ref[...] = jnp.zeros_like(acc_ref)
    r_ref[...] = jnp.zeros_like(r_ref)
    row = lax.broadcasted_iota(jnp.int32, (2 * Q_BLOCK, KEY_BLOCK), 0) & (Q_BLOCK - 1)
    col = lax.broadcasted_iota(jnp.int32, (2 * Q_BLOCK, KEY_BLOCK), 1)
    qpos = row + i * Q_BLOCK

    def cond(c):
        j, rmax = c
        return jnp.logical_and(j >= 0, rmax > SB_DEAD_LOGIT)

    def body(c):
        j, _ = c
        off = pl.multiple_of(j * KEY_BLOCK, KEY_BLOCK)
        valid = (col + j * KEY_BLOCK) < qpos
        pairs = range(n_pair)
        pvs, rns = _sb_blocks(
            [q2_ref[p] for p in pairs],
            [k_ref[pl.ds(off, KEY_BLOCK), p * LANES:(p + 1) * LANES] for p in pairs],
            [v_ref[pl.ds(off, KEY_BLOCK), p * LANES:(p + 1) * LANES] for p in pairs],
            valid, [r_ref[p] for p in pairs], uo_ref[...])
        rmax = jnp.float32(_NEG_INF)
        for p in pairs:
            acc_ref[p] += pvs[p]
            r_ref[p] = rns[p]
            rmax = jnp.maximum(rmax, jnp.max(rns[p]))
        return j - 1, rmax

    lax.while_loop(cond, body, (i, jnp.float32(0.0)))
    for p in range(n_pair):
        acc = acc_ref[p]
        o_ref[:, p * LANES:(p + 1) * LANES] = jnp.where(
            lane < HEAD_DIM_SB, acc[:Q_BLOCK], acc[Q_BLOCK:]).astype(o_ref.dtype)


def _sb_prompt(q, kb, vb, uo):
    s = q.shape[0]
    n_pair = WIDTH // LANES
    resident = lambda shape: pl.BlockSpec(shape, lambda i: (0, 0), pipeline_mode=pl.Buffered(1))
    return pl.pallas_call(
        _sb_prompt_body,
        grid=(s // Q_BLOCK,),
        in_specs=[pl.BlockSpec((Q_BLOCK, WIDTH), lambda i: (i, 0)),
                  resident((s, WIDTH)),
                  resident((s, WIDTH)),
                  resident((2 * KEY_BLOCK, 2 * KEY_BLOCK))],
        out_specs=pl.BlockSpec((Q_BLOCK, WIDTH), lambda i: (i, 0)),
        out_shape=jax.ShapeDtypeStruct((s, WIDTH), _BF),
        scratch_shapes=[pltpu.VMEM((n_pair, 2 * Q_BLOCK, LANES), _BF),
                        pltpu.VMEM((n_pair, 2 * Q_BLOCK, LANES), _F32),
                        pltpu.VMEM((n_pair, 2 * Q_BLOCK, LANES), _F32)],
        compiler_params=_params("parallel"),
        name="sb_prompt",
    )(q, kb, vb, uo)


def _sb_sample_body(q_ref, kn_ref, vn_ref, kp_ref, vp_ref, uo_ref, o_ref, acc_ref, r_ref):
    q = q_ref[...]
    tq = q.shape[0]
    row = lax.broadcasted_iota(jnp.int32, (tq, KEY_BLOCK), 0)
    col = lax.broadcasted_iota(jnp.int32, (tq, KEY_BLOCK), 1)
    (pv,), (r0,) = _sb_blocks([q], [kn_ref[...]], [vn_ref[...]], col < row,
                              [jnp.zeros((tq, LANES), _F32)], uo_ref[...])
    acc_ref[...] = pv
    r_ref[...] = r0
    always = col >= 0
    n_past = kp_ref.shape[0] // KEY_BLOCK

    def cond(c):
        j, rmax = c
        return jnp.logical_and(j >= 0, rmax > SB_DEAD_LOGIT)

    def body(c):
        j, _ = c
        off = pl.multiple_of(j * KEY_BLOCK, KEY_BLOCK)
        (pv,), (rn,) = _sb_blocks([q], [kp_ref[pl.ds(off, KEY_BLOCK), :].astype(_BF)],
                                  [vp_ref[pl.ds(off, KEY_BLOCK), :].astype(_BF)],
                                  always, [r_ref[...]], uo_ref[...])
        acc_ref[...] += pv
        r_ref[...] = rn
        return j - 1, jnp.max(rn)

    lax.while_loop(cond, body, (jnp.int32(n_past - 1), jnp.max(r0)))
    o_ref[...] = acc_ref[...]


def _sb_sample(q, kn, vn, kp, vp, uo):
    b, h, tq, dh = q.shape
    past = kp.shape[2]
    blk = lambda n: pl.BlockSpec((None, None, n, dh), lambda bi, hi: (bi, hi, 0, 0))
    return pl.pallas_call(
        _sb_sample_body,
        grid=(b, h),
        in_specs=[blk(tq), blk(KEY_BLOCK), blk(KEY_BLOCK), blk(past), blk(past),
                  pl.BlockSpec((2 * KEY_BLOCK, 2 * KEY_BLOCK), lambda bi, hi: (0, 0))],
        out_specs=blk(tq),
        out_shape=jax.ShapeDtypeStruct((b, h, tq, dh), _F32),
        scratch_shapes=[pltpu.VMEM((tq, dh), _F32), pltpu.VMEM((tq, LANES), _F32)],
        compiler_params=_params("parallel", "parallel"),
        name="sb_sample",
    )(q, kn, vn, kp, vp, uo)


def _mix_body(x_ref, oa_ref, g1_ref, w3_ref, bg_ref, gg_ref, wm_ref, bm_ref, wb_ref, wo_ref,
              g2_ref, y1_ref, h2t_ref, *gv_ref):
    x = x_ref[...]
    tt = x.shape[0]
    h = _rms(x, g1_ref[...]).astype(_BF)
    p = _dot(h, w3_ref[...])
    u = _gelu(p[:, :WIDTH])
    vn = _rms(_gelu(p[:, WIDTH:2 * WIDTH]), gg_ref[...])
    if gv_ref:
        gv_ref[0][...] = vn
    vnb = vn.astype(_BF)
    lane = lax.broadcasted_iota(jnp.int32, (MLP_CHUNK, LANES), 1)
    bias = bm_ref[...]
    chunks = []
    for c in range(tt // MLP_CHUNK):
        cols = []
        for m in range(WIDTH // LANES):
            vc = vnb[c * MLP_CHUNK:(c + 1) * MLP_CHUNK, m * LANES:(m + 1) * LANES]
            cols.append(jnp.where(lane < GROUP_DIM_MLP, _dot(wm_ref[2 * m], vc),
                                  _dot(wm_ref[2 * m + 1], vc)))
        chunks.append(jnp.concatenate(cols, axis=1) + bias)
    mixed = chunks[0] if len(chunks) == 1 else jnp.concatenate(chunks, axis=0)
    ob = (u * mixed).astype(_BF)
    gates = jax.nn.sigmoid(p[:, 2 * WIDTH:] + bg_ref[...])
    m = gates[:, :D_MODEL] * _dot(oa_ref[...], wb_ref[0]) + gates[:, D_MODEL:] * _dot(ob, wb_ref[1])
    y1 = x + _dot(m.astype(_BF), wo_ref[...])
    y1_ref[...] = y1
    h2t_ref[...] = _rms(y1, g2_ref[...]).T.astype(_BF)


def _mix(x, oa, g1, w3, bg, gg, wm, bm, wb, wo, g2, tt, emit_gv):
    t = x.shape[0]
    row = lambda i: (i, 0)
    fixed2 = lambda i: (0, 0)
    fixed3 = lambda i: (0, 0, 0)
    out_specs = [pl.BlockSpec((tt, D_MODEL), row), pl.BlockSpec((D_MODEL, tt), lambda i: (0, i))]
    out_shape = [jax.ShapeDtypeStruct((t, D_MODEL), _F32), jax.ShapeDtypeStruct((D_MODEL, t), _BF)]
    if emit_gv:
        out_specs.append(pl.BlockSpec((tt, WIDTH), row))
        out_shape.append(jax.ShapeDtypeStruct((t, WIDTH), _F32))
    return pl.pallas_call(
        _mix_body,
        grid=(t // tt,),
        in_specs=[pl.BlockSpec((tt, D_MODEL), row),
                  pl.BlockSpec((tt, WIDTH), row),
                  pl.BlockSpec((1, D_MODEL), fixed2),
                  pl.BlockSpec((D_MODEL, 2 * WIDTH + 2 * D_MODEL), fixed2),
                  pl.BlockSpec((1, 2 * D_MODEL), fixed2),
                  pl.BlockSpec((1, WIDTH), fixed2),
                  pl.BlockSpec((N_GROUPS_MLP, MLP_CHUNK, MLP_CHUNK), fixed3),
                  pl.BlockSpec((MLP_CHUNK, WIDTH), fixed2),
                  pl.BlockSpec((2, WIDTH, D_MODEL), fixed3),
                  pl.BlockSpec((D_MODEL, D_MODEL), fixed2),
                  pl.BlockSpec((1, D_MODEL), fixed2)],
        out_specs=out_specs,
        out_shape=out_shape,
        compiler_params=_params("parallel"),
        name="mixer_out",
    )(x, oa, g1, w3, bg, gg, wm, bm, wb, wo, g2)


def _top16(s):
    n = s.shape[0]
    row = lax.broadcasted_iota(jnp.int32, s.shape, 0)
    rank = jnp.full(s.shape, N_KEYS - 1, jnp.int32)
    vals = []
    x = s
    for a in range(PEER_TOPK):
        m = jnp.max(x, axis=0, keepdims=True)
        idx = jnp.min(jnp.where(x == m, row, n), axis=0, keepdims=True)
        hit = row == idx
        x = jnp.where(hit, _NEG_INF, x)
        rank = jnp.where(hit, a, rank)
        vals.append(m)
    return jnp.concatenate(vals, axis=0), rank


def _staircase(v1, v2):
    arow = lax.broadcasted_iota(jnp.int32, v1.shape, 0)
    count = jnp.zeros(v1.shape, jnp.int32)
    for _ in range(PEER_TOPK):
        nxt = jnp.full(v1.shape, _NEG_INF, _F32)
        for b in range(PEER_TOPK):
            nxt = jnp.where(count == b, v2[b:b + 1, :], nxt)
        cand = v1 + nxt
        m = jnp.max(cand, axis=0, keepdims=True)
        asel = jnp.min(jnp.where(cand == m, arow, PEER_TOPK), axis=0, keepdims=True)
        count = count + (arow == asel).astype(jnp.int32)
    return count


def _peer_body(h2t_ref, y1_ref, wqt_ref, sk_ref, u_ref, vt_ref, gf_ref, y_ref,
               sc_ref, lr_ref, f1_ref, r2_ref, e2_ref, st_ref, wt_ref, acc_ref):
    c = pl.program_id(1)
    tt = h2t_ref.shape[1]
    nlg = tt // LANES
    ec = u_ref.shape[0]
    n1_per_chunk = ec // N_KEYS
    h2t = h2t_ref[...]

    @pl.when(c == 0)
    def _route():
        acc_ref[...] = jnp.zeros_like(acc_ref)
        kd = 2 * N_KEYS
        for h in range(PEER_HEADS):
            qt = _dot(wqt_ref[h * kd:(h + 1) * kd, :], h2t).astype(_BF)
            for half in range(2):
                st = _dot(sk_ref[h, half], qt[half * N_KEYS:(half + 1) * N_KEYS, :])
                for lg in range(nlg):
                    sc_ref[h, half, lg] = st[:, lg * LANES:(lg + 1) * LANES]

        def route(idx, carry):
            h = idx // nlg
            lg = idx % nlg
            s1 = sc_ref[h, 0, lg]
            s2 = sc_ref[h, 1, lg]
            v1, rank1 = _top16(s1)
            v2, rank2 = _top16(s2)
            stair = _staircase(v1, v2)
            e1s = jnp.exp(v1 - v1[0:1, :])
            e2s = jnp.exp(v2 - v2[0:1, :])
            pref = jnp.zeros_like(e2s)
            for b in range(PEER_TOPK):
                pref = pref + jnp.where(stair > b, e2s[b:b + 1, :], 0.0)
            zsum = jnp.sum(e1s * pref, axis=0, keepdims=True)
            lr = jnp.zeros(s1.shape, jnp.int32)
            for a in range(PEER_TOPK):
                lr = jnp.where(rank1 == a, stair[a:a + 1, :], lr)
            lr_ref[h, lg] = lr.astype(_F32)
            f1_ref[h, lg] = jnp.exp(s1 - v1[0:1, :]) / zsum
            r2_ref[h, lg] = rank2.astype(_F32).astype(_BF)
            e2_ref[h, lg] = jnp.exp(s2 - v2[0:1, :]).astype(_BF)
            return carry

        lax.fori_loop(0, PEER_HEADS * nlg, route, 0, unroll=2)

    es = min(ec, PEER_SUB_EXPERTS)
    for sub in range(ec // es):
        st_ref[sub * es:(sub + 1) * es, :] = _dot(u_ref[sub * es:(sub + 1) * es, :], h2t)
    for sub in range(ec // es):
        for k in range(es // N_KEYS):
            n1 = c * n1_per_chunk + sub * (es // N_KEYS) + k
            r0 = sub * es + k * N_KEYS
            for lg in range(nlg):
                gate = jnp.zeros((N_KEYS, LANES), _BF)
                for h in range(PEER_HEADS):
                    lrow = lr_ref[h, lg, pl.ds(n1, 1), :].astype(_BF)
                    frow = f1_ref[h, lg, pl.ds(n1, 1), :].astype(_BF)
                    gate = gate + jnp.where(r2_ref[h, lg] < lrow, e2_ref[h, lg] * frow,
                                            jnp.zeros((), _BF))
                act = _gelu(st_ref[r0:r0 + N_KEYS, lg * LANES:(lg + 1) * LANES])
                wt_ref[r0:r0 + N_KEYS, lg * LANES:(lg + 1) * LANES] = gate * act.astype(_BF)
        acc_ref[...] += _dot(vt_ref[:, sub * es:(sub + 1) * es], wt_ref[sub * es:(sub + 1) * es, :])

    @pl.when(c == pl.num_programs(1) - 1)
    def _finish():
        y_ref[...] = _rms(y1_ref[...] + acc_ref[...].T, gf_ref[...])


def _peer(h2t, y1, wqt, sk, u, vt, gf, tt, ec):
    t = y1.shape[0]
    nlg = tt // LANES
    tile = lambda i, c: (i, 0)
    slab = lambda dt: pltpu.VMEM((PEER_HEADS, nlg, N_KEYS, LANES), dt)
    return pl.pallas_call(
        _peer_body,
        grid=(t // tt, N_EXPERTS // ec),
        in_specs=[pl.BlockSpec((D_MODEL, tt), lambda i, c: (0, i)),
                  pl.BlockSpec((tt, D_MODEL), tile),
                  pl.BlockSpec((PEER_HEADS * 2 * N_KEYS, D_MODEL), lambda i, c: (0, 0)),
                  pl.BlockSpec((PEER_HEADS, 2, N_KEYS, N_KEYS), lambda i, c: (0, 0, 0, 0)),
                  pl.BlockSpec((ec, D_MODEL), lambda i, c: (c, 0)),
                  pl.BlockSpec((D_MODEL, ec), lambda i, c: (0, c)),
                  pl.BlockSpec((1, D_MODEL), lambda i, c: (0, 0))],
        out_specs=pl.BlockSpec((tt, D_MODEL), tile),
        out_shape=jax.ShapeDtypeStruct((t, D_MODEL), _F32),
        scratch_shapes=[pltpu.VMEM((PEER_HEADS, 2, nlg, N_KEYS, LANES), _F32),
                        slab(_F32), slab(_F32), slab(_BF), slab(_BF),
                        pltpu.VMEM((ec, tt), _F32),
                        pltpu.VMEM((ec, tt), _BF),
                        pltpu.VMEM((D_MODEL, tt), _F32)],
        compiler_params=_params("parallel", "arbitrary"),
        name="peer_ffn",
    )(h2t, y1, wqt, sk, u, vt, gf)


def _to_heads(t2d, b):
    t = t2d.shape[0] // b
    return t2d.reshape(b, t, N_HEADS_SB, HEAD_DIM_SB).transpose(0, 2, 1, 3)


def kernel(x_prompt, x_sample, cache_k, cache_v, norm_mix_g, w_in, b_gate, gmlp_norm_g, w_s, b_s,
           w_branch, w_out, norm_ffn_g, w_query, sub_keys, expert_u, expert_v, norm_final_g):
    assert w_in.shape[0] == 1, "single trunk layer"
    bp, sp, _ = x_prompt.shape
    bs, ss, _ = x_sample.shape
    assert bp == 1 and sp % MLP_CHUNK == 0 and bs * ss == LANES and ss <= CHUNK

    w_in_b = w_in[0].astype(_BF)
    w_qkv = w_in_b[:, :3 * WIDTH]
    w_rest = w_in_b[:, 3 * WIDTH:]
    g_mix = norm_mix_g[0][None, :]
    g_ffn = norm_ffn_g[0][None, :]
    g_fin = norm_final_g[None, :]
    g_mlp = gmlp_norm_g[0][None, :]
    bg = b_gate[0][None, :]
    wb = w_branch[0].astype(_BF)
    wo = w_out[0].astype(_BF)
    wqt = w_query[0].T.astype(_BF)
    sk = sub_keys[0].astype(_BF)
    u_b = expert_u[0].astype(_BF)
    vt_b = expert_v[0].T.astype(_BF)
    jj = lax.broadcasted_iota(jnp.int32, (KEY_BLOCK, 2 * KEY_BLOCK), 0)
    cc = lax.broadcasted_iota(jnp.int32, (KEY_BLOCK, 2 * KEY_BLOCK), 1)
    uo = ((cc >= KEY_BLOCK) | (jj >= cc)).astype(_BF)
    uo = jnp.concatenate([uo, uo], axis=0)

    pos = jnp.arange(MLP_CHUNK)
    causal = (pos[None, :] // CHUNK) <= (pos[:, None] // CHUNK)
    wm_p = jnp.where(causal[None], w_s[0], 0.0).astype(_BF)
    bm_p = jnp.repeat(b_s[0].T, GROUP_DIM_MLP, axis=1)
    same_stream = (pos[None, :] // ss) == (pos[:, None] // ss)
    wm_s = jnp.where(same_stream[None], jnp.tile(w_s[0][:, :ss, :ss], (1, bs, bs)), 0.0).astype(_BF)
    bm_s = jnp.repeat(jnp.tile(b_s[0][:, :ss], (1, bs)).T, GROUP_DIM_MLP, axis=1)

    xp = x_prompt[0]
    q_p, k_p, v_p, kb_p, vb_p = _qkv(xp, g_mix, w_qkv, 512)
    oa_p = _sb_prompt(q_p, kb_p, vb_p, uo)
    y1_p, h2t_p = _mix(xp, oa_p, g_mix, w_rest, bg, g_mlp, wm_p, bm_p, wb, wo, g_ffn, 512, False)
    y_p = _peer(h2t_p, y1_p, wqt, sk, u_b, vt_b, g_fin, 512, 2048)

    xs = x_sample.reshape(bs * ss, D_MODEL)
    q_s, k_s, v_s, kb_s, vb_s = _qkv(xs, g_mix, w_qkv, LANES)
    pad = ((0, 0), (0, 0), (0, KEY_BLOCK - ss), (0, 0))
    oa_s = _sb_sample(_to_heads(q_s, bs), jnp.pad(_to_heads(kb_s, bs), pad),
                      jnp.pad(_to_heads(vb_s, bs), pad), cache_k[0], cache_v[0], uo)
    oa_s = oa_s.transpose(0, 2, 1, 3).reshape(bs * ss, WIDTH).astype(_BF)
    y1_s, h2t_s, gv_s = _mix(xs, oa_s, g_mix, w_rest, bg, g_mlp, wm_s, bm_s, wb, wo, g_ffn,
                             LANES, True)
    y_s = _peer(h2t_s, y1_s, wqt, sk, u_b, vt_b, g_fin, LANES, 2048)

    return (y_p[None],
            y_s.reshape(bs, ss, D_MODEL),
            _to_heads(k_p, bp)[None],
            _to_heads(v_p, bp)[None],
            _to_heads(k_s, bs)[None],
            _to_heads(v_s, bs)[None],
            gv_s.reshape(bs, ss, WIDTH)[None])
```

```python
import functools

import jax
import jax.numpy as jnp
from jax import lax
from jax.experimental import pallas as pl
from jax.experimental.pallas import tpu as pltpu

D_MODEL = 1024
N_HEADS_SB = 8
HEAD_DIM_SB = 64
WIDTH = N_HEADS_SB * HEAD_DIM_SB
N_GROUPS_MLP = 8
GROUP_DIM_MLP = WIDTH // N_GROUPS_MLP
MLP_CHUNK = 128
CHUNK = 64
PEER_HEADS = 8
N_KEYS = 128
N_EXPERTS = N_KEYS * N_KEYS
PEER_TOPK = 16
EPS = 1e-6
SB_SCALE = HEAD_DIM_SB ** -0.5

LANES = 128
KEY_BLOCK = LANES
Q_BLOCK = LANES
SB_DEAD_LOGIT = -120.0
VMEM_LIMIT_BYTES = 60 * 1024 * 1024
PEER_STEP_EXPERTS = 2048
PEER_SUB_EXPERTS = 512

_BF = jnp.bfloat16
_F32 = jnp.float32
_NEG_INF = float("-inf")


def _rms(x, g):
    return x * lax.rsqrt(jnp.mean(x * x, axis=-1, keepdims=True) + EPS) * g


def _gelu(x):
    c = 0.7978845608028654
    half = 0.5 * x
    return half + half * jnp.tanh(x * (c + (c * 0.044715) * (x * x)))


def _dot(a, b):
    return jnp.dot(a, b, preferred_element_type=_F32)


def _params(*sem):
    return pltpu.CompilerParams(dimension_semantics=sem, vmem_limit_bytes=VMEM_LIMIT_BYTES)


def _qkv_body(x_ref, g_ref, w_ref, q_ref, k_ref, v_ref, kb_ref, vb_ref):
    h = _rms(x_ref[...], g_ref[...]).astype(_BF)
    p = _dot(h, w_ref[...])
    q_ref[...] = (p[:, :WIDTH] * SB_SCALE).astype(_BF)
    k = p[:, WIDTH:2 * WIDTH]
    v = p[:, 2 * WIDTH:]
    for hd in range(N_HEADS_SB):
        k_ref[hd] = k[:, hd * HEAD_DIM_SB:(hd + 1) * HEAD_DIM_SB]
        v_ref[hd] = v[:, hd * HEAD_DIM_SB:(hd + 1) * HEAD_DIM_SB]
    kb_ref[...] = k.astype(_BF)
    vb_ref[...] = v.astype(_BF)


def _qkv(x, g, w_qkv, tt):
    t = x.shape[0]
    row = lambda i: (i, 0)
    fixed = lambda i: (0, 0)
    heads = pl.BlockSpec((N_HEADS_SB, tt, HEAD_DIM_SB), lambda i: (0, i, 0))
    return pl.pallas_call(
        _qkv_body,
        grid=(t // tt,),
        in_specs=[pl.BlockSpec((tt, D_MODEL), row),
                  pl.BlockSpec((1, D_MODEL), fixed),
                  pl.BlockSpec((D_MODEL, 3 * WIDTH), fixed)],
        out_specs=[pl.BlockSpec((tt, WIDTH), row), heads, heads,
                   pl.BlockSpec((tt, WIDTH), row), pl.BlockSpec((tt, WIDTH), row)],
        out_shape=[jax.ShapeDtypeStruct((t, WIDTH), _BF),
                   jax.ShapeDtypeStruct((N_HEADS_SB, t, HEAD_DIM_SB), _F32),
                   jax.ShapeDtypeStruct((N_HEADS_SB, t, HEAD_DIM_SB), _F32),
                   jax.ShapeDtypeStruct((t, WIDTH), _BF),
                   jax.ShapeDtypeStruct((t, WIDTH), _BF)],
        compiler_params=_params("parallel"),
        name="qkv_proj",
    )(x, g, w_qkv)


def _sb_blocks(qs, kbs, vbs, valid, rs, uo):
    zs = [lax.dot_general(q, kb, (((1,), (1,)), ((), ())), preferred_element_type=_F32)
          for q, kb in zip(qs, kbs)]
    hls = []
    for z in zs:
        softplus = jnp.maximum(z, 0.0) + jnp.log1p(jnp.exp(-jnp.abs(z)))
        lom = jnp.where(valid, -softplus, 0.0)
        hi = lom.astype(_BF)
        lo = (lom - hi.astype(_F32)).astype(_BF)
        hls.append(jnp.concatenate([hi, lo], axis=1))
    css = [_dot(hl, uo) for hl in hls]
    ws = [jnp.where(valid, jnp.exp(z + cs[:, :KEY_BLOCK] + r), 0.0).astype(_BF)
          for z, cs, r in zip(zs, css, rs)]
    pvs = [_dot(w, vb) for w, vb in zip(ws, vbs)]
    return pvs, [r + cs[:, KEY_BLOCK:] for r, cs in zip(rs, css)]


def _sb_prompt_body(q_ref, k_ref, v_ref, uo_ref, o_ref, q2_ref, acc_ref, r_ref):
    i = pl.program_id(0)
    n_pair = WIDTH // LANES
    lane = lax.broadcasted_iota(jnp.int32, (Q_BLOCK, LANES), 1)
    for p in range(n_pair):
        q = q_ref[:, p * LANES:(p + 1) * LANES]
        zero = jnp.zeros_like(q)
        q2_ref[p] = jnp.concatenate([jnp.where(lane < HEAD_DIM_SB, q, zero),
                                     jnp.where(lane >= HEAD_DIM_SB, q, zero)], axis=0)
    acc_ref[...] = jnp.zeros_like(acc_ref)
    r_ref[...] = jnp.zeros_like(r_ref)
    row = lax.broadcasted_iota(jnp.int32, (2 * Q_BLOCK, KEY_BLOCK), 0) & (Q_BLOCK - 1)
    col = lax.broadcasted_iota(jnp.int32, (2 * Q_BLOCK, KEY_BLOCK), 1)
    qpos = row + i * Q_BLOCK

    def cond(c):
        j, rmax = c
        return jnp.logical_and(j >= 0, rmax > SB_DEAD_LOGIT)

    def body(c):
        j, _ = c
        off = pl.multiple_of(j * KEY_BLOCK, KEY_BLOCK)
        valid = (col + j * KEY_BLOCK) < qpos
        pairs = range(n_pair)
        pvs, rns = _sb_blocks(
            [q2_ref[p] for p in pairs],
            [k_ref[pl.ds(off, KEY_BLOCK), p * LANES:(p + 1) * LANES] for p in pairs],
            [v_ref[pl.ds(off, KEY_BLOCK), p * LANES:(p + 1) * LANES] for p in pairs],
            valid, [r_ref[p] for p in pairs], uo_ref[...])
        rmax = jnp.float32(_NEG_INF)
        for p in pairs:
            acc_ref[p] += pvs[p]
            r_ref[p] = rns[p]
            rmax = jnp.maximum(rmax, jnp.max(rns[p]))
        return j - 1, rmax

    last_key_block = (i + 1) * (Q_BLOCK // KEY_BLOCK) - 1
    lax.while_loop(cond, body, (last_key_block, jnp.float32(0.0)))
    for p in range(n_pair):
        acc = acc_ref[p]
        o_ref[:, p * LANES:(p + 1) * LANES] = jnp.where(
            lane < HEAD_DIM_SB, acc[:Q_BLOCK], acc[Q_BLOCK:]).astype(o_ref.dtype)


def _sb_prompt(q, kb, vb, uo):
    s = q.shape[0]
    n_pair = WIDTH // LANES
    resident = lambda shape: pl.BlockSpec(shape, lambda i: (0, 0), pipeline_mode=pl.Buffered(1))
    return pl.pallas_call(
        _sb_prompt_body,
        grid=(s // Q_BLOCK,),
        in_specs=[pl.BlockSpec((Q_BLOCK, WIDTH), lambda i: (i, 0)),
                  resident((s, WIDTH)),
                  resident((s, WIDTH)),
                  resident((2 * KEY_BLOCK, 2 * KEY_BLOCK))],
        out_specs=pl.BlockSpec((Q_BLOCK, WIDTH), lambda i: (i, 0)),
        out_shape=jax.ShapeDtypeStruct((s, WIDTH), _BF),
        scratch_shapes=[pltpu.VMEM((n_pair, 2 * Q_BLOCK, LANES), _BF),
                        pltpu.VMEM((n_pair, 2 * Q_BLOCK, LANES), _F32),
                        pltpu.VMEM((n_pair, 2 * Q_BLOCK, LANES), _F32)],
        compiler_params=_params("parallel"),
        name="sb_prompt",
    )(q, kb, vb, uo)


def _sb_sample_body(q_ref, kn_ref, vn_ref, kp_ref, vp_ref, uo_ref, o_ref, acc_ref, r_ref):
    n_head, tq, _ = q_ref.shape
    heads = range(n_head)
    qs = [q_ref[hd] for hd in heads]
    row = lax.broadcasted_iota(jnp.int32, (tq, KEY_BLOCK), 0)
    col = lax.broadcasted_iota(jnp.int32, (tq, KEY_BLOCK), 1)
    pvs, r0s = _sb_blocks(qs, [kn_ref[hd] for hd in heads], [vn_ref[hd] for hd in heads], col < row,
                          [jnp.zeros((tq, LANES), _F32) for _ in heads], uo_ref[...])
    rmax0 = jnp.float32(_NEG_INF)
    for hd in heads:
        acc_ref[hd] = pvs[hd]
        r_ref[hd] = r0s[hd]
        rmax0 = jnp.maximum(rmax0, jnp.max(r0s[hd]))
    always = col >= 0
    n_past = kp_ref.shape[1] // KEY_BLOCK

    def cond(c):
        j, rmax = c
        return jnp.logical_and(j >= 0, rmax > SB_DEAD_LOGIT)

    def body(c):
        j, _ = c
        off = pl.multiple_of(j * KEY_BLOCK, KEY_BLOCK)
        pvs, rns = _sb_blocks(qs, [kp_ref[hd, pl.ds(off, KEY_BLOCK), :].astype(_BF) for hd in heads],
                              [vp_ref[hd, pl.ds(off, KEY_BLOCK), :].astype(_BF) for hd in heads],
                              always, [r_ref[hd] for hd in heads], uo_ref[...])
        rmax = jnp.float32(_NEG_INF)
        for hd in heads:
            acc_ref[hd] += pvs[hd]
            r_ref[hd] = rns[hd]
            rmax = jnp.maximum(rmax, jnp.max(rns[hd]))
        return j - 1, rmax

    lax.while_loop(cond, body, (jnp.int32(n_past - 1), rmax0))
    o_ref[...] = acc_ref[...]


def _sb_sample(q, kn, vn, kp, vp, uo):
    b, h, tq, dh = q.shape
    past = kp.shape[2]
    blk = lambda n: pl.BlockSpec((None, h, n, dh), lambda bi: (bi, 0, 0, 0))
    return pl.pallas_call(
        _sb_sample_body,
        grid=(b,),
        in_specs=[blk(tq), blk(KEY_BLOCK), blk(KEY_BLOCK), blk(past), blk(past),
                  pl.BlockSpec((2 * KEY_BLOCK, 2 * KEY_BLOCK), lambda bi: (0, 0))],
        out_specs=blk(tq),
        out_shape=jax.ShapeDtypeStruct((b, h, tq, dh), _F32),
        scratch_shapes=[pltpu.VMEM((h, tq, dh), _F32), pltpu.VMEM((h, tq, LANES), _F32)],
        compiler_params=_params("parallel"),
        name="sb_sample",
    )(q, kn, vn, kp, vp, uo)


def _mix_body(x_ref, oa_ref, g1_ref, w3_ref, bg_ref, gg_ref, wm_ref, bm_ref, wb_ref, wo_ref,
              g2_ref, y1_ref, h2t_ref, *gv_ref):
    x = x_ref[...]
    tt = x.shape[0]
    h = _rms(x, g1_ref[...]).astype(_BF)
    p = _dot(h, w3_ref[...])
    u = _gelu(p[:, :WIDTH])
    vn = _rms(_gelu(p[:, WIDTH:2 * WIDTH]), gg_ref[...])
    if gv_ref:
        gv_ref[0][...] = vn
    vnb = vn.astype(_BF)
    lane = lax.broadcasted_iota(jnp.int32, (MLP_CHUNK, LANES), 1)
    bias = bm_ref[...]
    chunks = []
    for c in range(tt // MLP_CHUNK):
        cols = []
        for m in range(WIDTH // LANES):
            vc = vnb[c * MLP_CHUNK:(c + 1) * MLP_CHUNK, m * LANES:(m + 1) * LANES]
            cols.append(jnp.where(lane < GROUP_DIM_MLP, _dot(wm_ref[2 * m], vc),
                                  _dot(wm_ref[2 * m + 1], vc)))
        chunks.append(jnp.concatenate(cols, axis=1) + bias)
    mixed = chunks[0] if len(chunks) == 1 else jnp.concatenate(chunks, axis=0)
    ob = (u * mixed).astype(_BF)
    gates = jax.nn.sigmoid(p[:, 2 * WIDTH:] + bg_ref[...])
    m = gates[:, :D_MODEL] * _dot(oa_ref[...], wb_ref[0]) + gates[:, D_MODEL:] * _dot(ob, wb_ref[1])
    y1 = x + _dot(m.astype(_BF), wo_ref[...])
    y1_ref[...] = y1
    h2t_ref[...] = _rms(y1, g2_ref[...]).T.astype(_BF)


def _mix(x, oa, g1, w3, bg, gg, wm, bm, wb, wo, g2, tt, emit_gv):
    t = x.shape[0]
    row = lambda i: (i, 0)
    fixed2 = lambda i: (0, 0)
    fixed3 = lambda i: (0, 0, 0)
    out_specs = [pl.BlockSpec((tt, D_MODEL), row), pl.BlockSpec((D_MODEL, tt), lambda i: (0, i))]
    out_shape = [jax.ShapeDtypeStruct((t, D_MODEL), _F32), jax.ShapeDtypeStruct((D_MODEL, t), _BF)]
    if emit_gv:
        out_specs.append(pl.BlockSpec((tt, WIDTH), row))
        out_shape.append(jax.ShapeDtypeStruct((t, WIDTH), _F32))
    return pl.pallas_call(
        _mix_body,
        grid=(t // tt,),
        in_specs=[pl.BlockSpec((tt, D_MODEL), row),
                  pl.BlockSpec((tt, WIDTH), row),
                  pl.BlockSpec((1, D_MODEL), fixed2),
                  pl.BlockSpec((D_MODEL, 2 * WIDTH + 2 * D_MODEL), fixed2),
                  pl.BlockSpec((1, 2 * D_MODEL), fixed2),
                  pl.BlockSpec((1, WIDTH), fixed2),
                  pl.BlockSpec((N_GROUPS_MLP, MLP_CHUNK, MLP_CHUNK), fixed3),
                  pl.BlockSpec((MLP_CHUNK, WIDTH), fixed2),
                  pl.BlockSpec((2, WIDTH, D_MODEL), fixed3),
                  pl.BlockSpec((D_MODEL, D_MODEL), fixed2),
                  pl.BlockSpec((1, D_MODEL), fixed2)],
        out_specs=out_specs,
        out_shape=out_shape,
        compiler_params=_params("parallel"),
        name="mixer_out",
    )(x, oa, g1, w3, bg, gg, wm, bm, wb, wo, g2)


def _top16(xs, exact):
    shape = xs[0].shape
    row = lax.broadcasted_iota(jnp.int32, shape, 0)
    ranks = [jnp.full(shape, N_KEYS - 1, jnp.int32) for _ in xs]
    vals = [[] for _ in xs]
    for a in range(PEER_TOPK):
        ms = [jnp.max(x, axis=0, keepdims=True) for x in xs]
        if exact:
            idxs = [jnp.min(jnp.where(x == m, row, N_KEYS), axis=0, keepdims=True)
                    for x, m in zip(xs, ms)]
            hits = [row == idx for idx in idxs]
        else:
            hits = [x == m for x, m in zip(xs, ms)]
        xs = [jnp.where(hit, _NEG_INF, x) for hit, x in zip(hits, xs)]
        ranks = [jnp.where(hit, a, rank) for hit, rank in zip(hits, ranks)]
        for v, m in zip(vals, ms):
            v.append(m)
    removed = [jnp.sum((x == _NEG_INF).astype(jnp.int32), axis=0, keepdims=True) for x in xs]
    return [jnp.concatenate(v, axis=0) for v in vals], ranks, removed


def _staircase(v1s, v2s):
    shape = v1s[0].shape
    arow = lax.broadcasted_iota(jnp.int32, shape, 0)
    counts = [jnp.zeros(shape, jnp.int32) for _ in v1s]
    for _ in range(PEER_TOPK):
        nxts = []
        for count, v2 in zip(counts, v2s):
            nxt = jnp.full(shape, _NEG_INF, _F32)
            for b in range(PEER_TOPK):
                nxt = jnp.where(count == b, v2[b:b + 1, :], nxt)
            nxts.append(nxt)
        cands = [v1 + nxt for v1, nxt in zip(v1s, nxts)]
        ms = [jnp.max(cand, axis=0, keepdims=True) for cand in cands]
        asels = [jnp.min(jnp.where(cand == m, arow, PEER_TOPK), axis=0, keepdims=True)
                 for cand, m in zip(cands, ms)]
        counts = [count + (arow == asel).astype(jnp.int32) for count, asel in zip(counts, asels)]
    return counts


def _peer_body(h2t_ref, y1_ref, wqt_ref, sk_ref, u_ref, vt_ref, gf_ref, y_ref,
               sc_ref, lr_ref, f1_ref, r2_ref, e2_ref, st_ref, wt_ref, acc_ref):
    c = pl.program_id(1)
    tt = h2t_ref.shape[1]
    nlg = tt // LANES
    ec = u_ref.shape[0]
    n1_per_chunk = ec // N_KEYS
    h2t = h2t_ref[...]

    @pl.when(c == 0)
    def _route():
        acc_ref[...] = jnp.zeros_like(acc_ref)
        kd = 2 * N_KEYS
        for h in range(PEER_HEADS):
            qt = _dot(wqt_ref[h * kd:(h + 1) * kd, :], h2t).astype(_BF)
            for half in range(2):
                st = _dot(sk_ref[h, half], qt[half * N_KEYS:(half + 1) * N_KEYS, :])
                for lg in range(nlg):
                    sc_ref[h, half, lg] = st[:, lg * LANES:(lg + 1) * LANES]

        def route_pair(idx, exact):
            slabs = [2 * idx, 2 * idx + 1]
            hs = [s // nlg for s in slabs]
            lgs = [s % nlg for s in slabs]
            s1s = [sc_ref[h, 0, lg] for h, lg in zip(hs, lgs)]
            s2s = [sc_ref[h, 1, lg] for h, lg in zip(hs, lgs)]
            vals, ranks, removed = _top16(s1s + s2s, exact)
            v1s, v2s = vals[:2], vals[2:]
            stairs = _staircase(v1s, v2s)
            for i in range(2):
                h, lg, v1, v2, stair = hs[i], lgs[i], v1s[i], v2s[i], stairs[i]
                e1s = jnp.exp(v1 - v1[0:1, :])
                e2s = jnp.exp(v2 - v2[0:1, :])
                pref = jnp.zeros_like(e2s)
                for b in range(PEER_TOPK):
                    pref = pref + jnp.where(stair > b, e2s[b:b + 1, :], 0.0)
                zsum = jnp.sum(e1s * pref, axis=0, keepdims=True)
                lr = jnp.zeros(s1s[i].shape, jnp.int32)
                for a in range(PEER_TOPK):
                    lr = jnp.where(ranks[i] == a, stair[a:a + 1, :], lr)
                blocked = lr_ref.shape[2:]
                lr_ref[h, lg] = lr.astype(_F32).reshape(blocked)
                f1_ref[h, lg] = (jnp.exp(s1s[i] - v1[0:1, :]) / zsum).reshape(blocked)
                r2_ref[h, lg] = ranks[2 + i].astype(_F32).astype(_BF)
                e2_ref[h, lg] = jnp.exp(s2s[i] - v2[0:1, :]).astype(_BF)
            return removed

        def route(idx, carry):
            removed = route_pair(idx, exact=False)
            tied = sum(jnp.sum((r != PEER_TOPK).astype(jnp.int32)) for r in removed)

            @pl.when(tied > 0)
            def _redo():
                route_pair(idx, exact=True)

            return carry

        lax.fori_loop(0, (PEER_HEADS * nlg) // 2, route, 0)

    es = min(ec, PEER_SUB_EXPERTS)
    for sub in range(ec // es):
        st_ref[sub * es:(sub + 1) * es, :] = _dot(u_ref[sub * es:(sub + 1) * es, :], h2t)
    for sub in range(ec // es):
        for k in range(es // N_KEYS):
            n1_local = sub * (es // N_KEYS) + k
            r0 = sub * es + k * N_KEYS
            for lg in range(nlg):
                gate = jnp.zeros((N_KEYS, LANES), _BF)
                for h in range(PEER_HEADS):
                    lrow = lr_ref[h, lg, c, n1_local:n1_local + 1, :].astype(_BF)
                    frow = f1_ref[h, lg, c, n1_local:n1_local + 1, :].astype(_BF)
                    gate = gate + jnp.where(r2_ref[h, lg] < lrow, e2_ref[h, lg] * frow,
                                            jnp.zeros((), _BF))
                act = _gelu(st_ref[r0:r0 + N_KEYS, lg * LANES:(lg + 1) * LANES])
                wt_ref[r0:r0 + N_KEYS, lg * LANES:(lg + 1) * LANES] = gate * act.astype(_BF)
        acc_ref[...] += _dot(vt_ref[:, sub * es:(sub + 1) * es], wt_ref[sub * es:(sub + 1) * es, :])

    @pl.when(c == pl.num_programs(1) - 1)
    def _finish():
        y_ref[...] = _rms(y1_ref[...] + acc_ref[...].T, gf_ref[...])


def _peer(h2t, y1, wqt, sk, u, vt, gf, tt):
    t = y1.shape[0]
    nlg = tt // LANES
    ec = PEER_STEP_EXPERTS
    assert N_EXPERTS % ec == 0 and (PEER_HEADS * nlg) % 2 == 0
    tile = lambda i, c: (i, 0)
    once = dict(pipeline_mode=pl.Buffered(1))
    slab = lambda dt: pltpu.VMEM((PEER_HEADS, nlg, N_KEYS, LANES), dt)
    rows = lambda: pltpu.VMEM((PEER_HEADS, nlg, N_EXPERTS // ec, ec // N_KEYS, LANES), _F32)
    return pl.pallas_call(
        _peer_body,
        grid=(t // tt, N_EXPERTS // ec),
        in_specs=[pl.BlockSpec((D_MODEL, tt), lambda i, c: (0, i)),
                  pl.BlockSpec((tt, D_MODEL), tile),
                  pl.BlockSpec((PEER_HEADS * 2 * N_KEYS, D_MODEL), lambda i, c: (0, 0), **once),
                  pl.BlockSpec((PEER_HEADS, 2, N_KEYS, N_KEYS), lambda i, c: (0, 0, 0, 0), **once),
                  pl.BlockSpec((ec, D_MODEL), lambda i, c: (c, 0)),
                  pl.BlockSpec((D_MODEL, ec), lambda i, c: (0, c)),
                  pl.BlockSpec((1, D_MODEL), lambda i, c: (0, 0))],
        out_specs=pl.BlockSpec((tt, D_MODEL), tile),
        out_shape=jax.ShapeDtypeStruct((t, D_MODEL), _F32),
        scratch_shapes=[pltpu.VMEM((PEER_HEADS, 2, nlg, N_KEYS, LANES), _F32),
                        rows(), rows(), slab(_BF), slab(_BF),
                        pltpu.VMEM((ec, tt), _F32),
                        pltpu.VMEM((ec, tt), _BF),
                        pltpu.VMEM((D_MODEL, tt), _F32)],
        compiler_params=_params("parallel", "arbitrary"),
        name="peer_ffn",
    )(h2t, y1, wqt, sk, u, vt, gf)


def _to_heads(t2d, b):
    t = t2d.shape[0] // b
    return t2d.reshape(b, t, N_HEADS_SB, HEAD_DIM_SB).transpose(0, 2, 1, 3)


def _split_streams(heads_first, b):
    h, bt, dh = heads_first.shape
    return heads_first.reshape(h, b, bt // b, dh).transpose(1, 0, 2, 3)


def kernel(x_prompt, x_sample, cache_k, cache_v, norm_mix_g, w_in, b_gate, gmlp_norm_g, w_s, b_s,
           w_branch, w_out, norm_ffn_g, w_query, sub_keys, expert_u, expert_v, norm_final_g):
    assert w_in.shape[0] == 1, "single trunk layer"
    bp, sp, _ = x_prompt.shape
    bs, ss, _ = x_sample.shape
    assert bp == 1 and sp % MLP_CHUNK == 0 and bs * ss == LANES and ss <= CHUNK

    w_in_b = w_in[0].astype(_BF)
    w_qkv = w_in_b[:, :3 * WIDTH]
    w_rest = w_in_b[:, 3 * WIDTH:]
    g_mix = norm_mix_g[0][None, :]
    g_ffn = norm_ffn_g[0][None, :]
    g_fin = norm_final_g[None, :]
    g_mlp = gmlp_norm_g[0][None, :]
    bg = b_gate[0][None, :]
    wb = w_branch[0].astype(_BF)
    wo = w_out[0].astype(_BF)
    wqt = w_query[0].T.astype(_BF)
    sk = sub_keys[0].astype(_BF)
    u_b = expert_u[0].astype(_BF)
    vt_b = expert_v[0].T.astype(_BF)
    jj = lax.broadcasted_iota(jnp.int32, (KEY_BLOCK, 2 * KEY_BLOCK), 0)
    cc = lax.broadcasted_iota(jnp.int32, (KEY_BLOCK, 2 * KEY_BLOCK), 1)
    uo = ((cc >= KEY_BLOCK) | (jj >= cc)).astype(_BF)
    uo = jnp.concatenate([uo, uo], axis=0)

    pos = jnp.arange(MLP_CHUNK)
    causal = (pos[None, :] // CHUNK) <= (pos[:, None] // CHUNK)
    wm_p = jnp.where(causal[None], w_s[0], 0.0).astype(_BF)
    bm_p = jnp.repeat(b_s[0].T, GROUP_DIM_MLP, axis=1)
    same_stream = (pos[None, :] // ss) == (pos[:, None] // ss)
    wm_s = jnp.where(same_stream[None], jnp.tile(w_s[0][:, :ss, :ss], (1, bs, bs)), 0.0).astype(_BF)
    bm_s = jnp.repeat(jnp.tile(b_s[0][:, :ss], (1, bs)).T, GROUP_DIM_MLP, axis=1)

    xp = x_prompt[0]
    q_p, k_p, v_p, kb_p, vb_p = _qkv(xp, g_mix, w_qkv, 512)
    oa_p = _sb_prompt(q_p, kb_p, vb_p, uo)
    y1_p, h2t_p = _mix(xp, oa_p, g_mix, w_rest, bg, g_mlp, wm_p, bm_p, wb, wo, g_ffn, 512, False)
    y_p = _peer(h2t_p, y1_p, wqt, sk, u_b, vt_b, g_fin, 512)

    xs = x_sample.reshape(bs * ss, D_MODEL)
    q_s, k_s, v_s, kb_s, vb_s = _qkv(xs, g_mix, w_qkv, LANES)
    pad = ((0, 0), (0, 0), (0, KEY_BLOCK - ss), (0, 0))
    oa_s = _sb_sample(_to_heads(q_s, bs), jnp.pad(_to_heads(kb_s, bs), pad),
                      jnp.pad(_to_heads(vb_s, bs), pad), cache_k[0], cache_v[0], uo)
    oa_s = oa_s.transpose(0, 2, 1, 3).reshape(bs * ss, WIDTH).astype(_BF)
    y1_s, h2t_s, gv_s = _mix(xs, oa_s, g_mix, w_rest, bg, g_mlp, wm_s, bm_s, wb, wo, g_ffn,
                             LANES, True)
    y_s = _peer(h2t_s, y1_s, wqt, sk, u_b, vt_b, g_fin, LANES)

    return (y_p[None],
            y_s.reshape(bs, ss, D_MODEL),
            _split_streams(k_p, bp)[None],
            _split_streams(v_p, bp)[None],
            _split_streams(k_s, bs)[None],
            _split_streams(v_s, bs)[None],
            gv_s.reshape(bs, ss, WIDTH)[None])
```

```python
import functools

import jax
import jax.numpy as jnp
from jax import lax
from jax.experimental import pallas as pl
from jax.experimental.pallas import tpu as pltpu

D_MODEL = 1024
N_HEADS_SB = 8
HEAD_DIM_SB = 64
WIDTH = N_HEADS_SB * HEAD_DIM_SB
N_GROUPS_MLP = 8
GROUP_DIM_MLP = WIDTH // N_GROUPS_MLP
MLP_CHUNK = 128
CHUNK = 64
PEER_HEADS = 8
N_KEYS = 128
N_EXPERTS = N_KEYS * N_KEYS
PEER_TOPK = 16
EPS = 1e-6
SB_SCALE = HEAD_DIM_SB ** -0.5

LANES = 128
KEY_BLOCK = LANES
Q_BLOCK = LANES
SB_DEAD_LOGIT = -120.0
VMEM_LIMIT_BYTES = 60 * 1024 * 1024
PEER_STEP_EXPERTS = 2048
PEER_SUB_EXPERTS = 512

_BF = jnp.bfloat16
_F32 = jnp.float32
_NEG_INF = float("-inf")


def _rms(x, g):
    return x * lax.rsqrt(jnp.mean(x * x, axis=-1, keepdims=True) + EPS) * g


def _gelu(x):
    c = 0.7978845608028654
    half = 0.5 * x
    return half + half * jnp.tanh(x * (c + (c * 0.044715) * (x * x)))


def _dot(a, b):
    return jnp.dot(a, b, preferred_element_type=_F32)


def _params(*sem):
    return pltpu.CompilerParams(dimension_semantics=sem, vmem_limit_bytes=VMEM_LIMIT_BYTES)


def _qkv_body(x_ref, g_ref, w_ref, q_ref, k_ref, v_ref, kb_ref, vb_ref):
    h = _rms(x_ref[...], g_ref[...]).astype(_BF)
    p = _dot(h, w_ref[...])
    q_ref[...] = (p[:, :WIDTH] * SB_SCALE).astype(_BF)
    k = p[:, WIDTH:2 * WIDTH]
    v = p[:, 2 * WIDTH:]
    k_ref[...] = k.T
    v_ref[...] = v.T
    kb_ref[...] = k.astype(_BF)
    vb_ref[...] = v.astype(_BF)


def _qkv(x, g, w_qkv, tt):
    t = x.shape[0]
    row = lambda i: (i, 0)
    fixed = lambda i: (0, 0)
    heads = pl.BlockSpec((WIDTH, tt), lambda i: (0, i))
    return pl.pallas_call(
        _qkv_body,
        grid=(t // tt,),
        in_specs=[pl.BlockSpec((tt, D_MODEL), row),
                  pl.BlockSpec((1, D_MODEL), fixed),
                  pl.BlockSpec((D_MODEL, 3 * WIDTH), fixed)],
        out_specs=[pl.BlockSpec((tt, WIDTH), row), heads, heads,
                   pl.BlockSpec((tt, WIDTH), row), pl.BlockSpec((tt, WIDTH), row)],
        out_shape=[jax.ShapeDtypeStruct((t, WIDTH), _BF),
                   jax.ShapeDtypeStruct((WIDTH, t), _F32),
                   jax.ShapeDtypeStruct((WIDTH, t), _F32),
                   jax.ShapeDtypeStruct((t, WIDTH), _BF),
                   jax.ShapeDtypeStruct((t, WIDTH), _BF)],
        compiler_params=_params("parallel"),
        name="qkv_proj",
    )(x, g, w_qkv)


def _sb_blocks(qs, kbs, vbs, valid, rs, uo, keys_minor=False):
    contract_last = (((1,), (1,)), ((), ()))
    if keys_minor:
        zs = [_dot(q, kb) for q, kb in zip(qs, kbs)]
    else:
        zs = [lax.dot_general(q, kb, contract_last, preferred_element_type=_F32)
              for q, kb in zip(qs, kbs)]
    hls = []
    for z in zs:
        softplus = jnp.maximum(z, 0.0) + jnp.log1p(jnp.exp(-jnp.abs(z)))
        lom = jnp.where(valid, -softplus, 0.0)
        hi = lom.astype(_BF)
        lo = (lom - hi.astype(_F32)).astype(_BF)
        hls.append(jnp.concatenate([hi, lo], axis=1))
    css = [_dot(hl, uo) for hl in hls]
    ws = [jnp.where(valid, jnp.exp(z + cs[:, :KEY_BLOCK] + r), 0.0).astype(_BF)
          for z, cs, r in zip(zs, css, rs)]
    if keys_minor:
        pvs = [lax.dot_general(w, vb, contract_last, preferred_element_type=_F32)
               for w, vb in zip(ws, vbs)]
    else:
        pvs = [_dot(w, vb) for w, vb in zip(ws, vbs)]
    return pvs, [r + cs[:, KEY_BLOCK:] for r, cs in zip(rs, css)]


def _sb_prompt_body(q_ref, k_ref, v_ref, uo_ref, o_ref, q2_ref, acc_ref, r_ref):
    i = pl.program_id(0)
    n_pair = WIDTH // LANES
    lane = lax.broadcasted_iota(jnp.int32, (Q_BLOCK, LANES), 1)
    for p in range(n_pair):
        q = q_ref[:, p * LANES:(p + 1) * LANES]
        zero = jnp.zeros_like(q)
        q2_ref[p] = jnp.concatenate([jnp.where(lane < HEAD_DIM_SB, q, zero),
                                     jnp.where(lane >= HEAD_DIM_SB, q, zero)], axis=0)
    acc_ref[...] = jnp.zeros_like(acc_ref)
    r_ref[...] = jnp.zeros_like(r_ref)
    row = lax.broadcasted_iota(jnp.int32, (2 * Q_BLOCK, KEY_BLOCK), 0) & (Q_BLOCK - 1)
    col = lax.broadcasted_iota(jnp.int32, (2 * Q_BLOCK, KEY_BLOCK), 1)
    qpos = row + i * Q_BLOCK

    def cond(c):
        j, rmax = c
        return jnp.logical_and(j >= 0, rmax > SB_DEAD_LOGIT)

    def body(c):
        j, _ = c
        off = pl.multiple_of(j * KEY_BLOCK, KEY_BLOCK)
        valid = (col + j * KEY_BLOCK) < qpos
        pairs = range(n_pair)
        pvs, rns = _sb_blocks(
            [q2_ref[p] for p in pairs],
            [k_ref[pl.ds(off, KEY_BLOCK), p * LANES:(p + 1) * LANES] for p in pairs],
            [v_ref[pl.ds(off, KEY_BLOCK), p * LANES:(p + 1) * LANES] for p in pairs],
            valid, [r_ref[p] for p in pairs], uo_ref[...])
        rmax = jnp.float32(_NEG_INF)
        for p in pairs:
            acc_ref[p] += pvs[p]
            r_ref[p] = rns[p]
            rmax = jnp.maximum(rmax, jnp.max(rns[p]))
        return j - 1, rmax

    last_key_block = (i + 1) * (Q_BLOCK // KEY_BLOCK) - 1
    lax.while_loop(cond, body, (last_key_block, jnp.float32(0.0)))
    for p in range(n_pair):
        acc = acc_ref[p]
        o_ref[:, p * LANES:(p + 1) * LANES] = jnp.where(
            lane < HEAD_DIM_SB, acc[:Q_BLOCK], acc[Q_BLOCK:]).astype(o_ref.dtype)


def _sb_prompt(q, kb, vb, uo):
    s = q.shape[0]
    n_pair = WIDTH // LANES
    resident = lambda shape: pl.BlockSpec(shape, lambda i: (0, 0), pipeline_mode=pl.Buffered(1))
    return pl.pallas_call(
        _sb_prompt_body,
        grid=(s // Q_BLOCK,),
        in_specs=[pl.BlockSpec((Q_BLOCK, WIDTH), lambda i: (i, 0)),
                  resident((s, WIDTH)),
                  resident((s, WIDTH)),
                  resident((2 * KEY_BLOCK, 2 * KEY_BLOCK))],
        out_specs=pl.BlockSpec((Q_BLOCK, WIDTH), lambda i: (i, 0)),
        out_shape=jax.ShapeDtypeStruct((s, WIDTH), _BF),
        scratch_shapes=[pltpu.VMEM((n_pair, 2 * Q_BLOCK, LANES), _BF),
                        pltpu.VMEM((n_pair, 2 * Q_BLOCK, LANES), _F32),
                        pltpu.VMEM((n_pair, 2 * Q_BLOCK, LANES), _F32)],
        compiler_params=_params("parallel"),
        name="sb_prompt",
    )(q, kb, vb, uo)


def _sb_sample_body(q_ref, kn_ref, vn_ref, kp_ref, vp_ref, uo_ref, o_ref, acc_ref, r_ref):
    n_head, tq, _ = q_ref.shape
    heads = range(n_head)
    qs = [q_ref[hd] for hd in heads]
    row = lax.broadcasted_iota(jnp.int32, (tq, KEY_BLOCK), 0)
    col = lax.broadcasted_iota(jnp.int32, (tq, KEY_BLOCK), 1)
    pvs, r0s = _sb_blocks(qs, [kn_ref[hd] for hd in heads], [vn_ref[hd] for hd in heads], col < row,
                          [jnp.zeros((tq, LANES), _F32) for _ in heads], uo_ref[...],
                          keys_minor=True)
    rmax0 = jnp.float32(_NEG_INF)
    for hd in heads:
        acc_ref[hd] = pvs[hd]
        r_ref[hd] = r0s[hd]
        rmax0 = jnp.maximum(rmax0, jnp.max(r0s[hd]))
    always = col >= 0
    n_past = kp_ref.shape[2] // KEY_BLOCK

    def cond(c):
        j, rmax = c
        return jnp.logical_and(j >= 0, rmax > SB_DEAD_LOGIT)

    def body(c):
        j, _ = c
        off = pl.multiple_of(j * KEY_BLOCK, KEY_BLOCK)
        pvs, rns = _sb_blocks(qs, [kp_ref[hd, :, pl.ds(off, KEY_BLOCK)].astype(_BF) for hd in heads],
                              [vp_ref[hd, :, pl.ds(off, KEY_BLOCK)].astype(_BF) for hd in heads],
                              always, [r_ref[hd] for hd in heads], uo_ref[...], keys_minor=True)
        rmax = jnp.float32(_NEG_INF)
        for hd in heads:
            acc_ref[hd] += pvs[hd]
            r_ref[hd] = rns[hd]
            rmax = jnp.maximum(rmax, jnp.max(rns[hd]))
        return j - 1, rmax

    lax.while_loop(cond, body, (jnp.int32(n_past - 1), rmax0))
    o_ref[...] = acc_ref[...]


def _sb_sample(q, kn, vn, kp, vp, uo):
    b, h, tq, dh = q.shape
    past = kp.shape[3]
    blk = lambda n: pl.BlockSpec((None, h, n, dh), lambda bi: (bi, 0, 0, 0))
    blk_t = lambda n: pl.BlockSpec((None, h, dh, n), lambda bi: (bi, 0, 0, 0))
    return pl.pallas_call(
        _sb_sample_body,
        grid=(b,),
        in_specs=[blk(tq), blk_t(KEY_BLOCK), blk_t(KEY_BLOCK), blk_t(past), blk_t(past),
                  pl.BlockSpec((2 * KEY_BLOCK, 2 * KEY_BLOCK), lambda bi: (0, 0))],
        out_specs=blk(tq),
        out_shape=jax.ShapeDtypeStruct((b, h, tq, dh), _F32),
        scratch_shapes=[pltpu.VMEM((h, tq, dh), _F32), pltpu.VMEM((h, tq, LANES), _F32)],
        compiler_params=_params("parallel"),
        name="sb_sample",
    )(q, kn, vn, kp, vp, uo)


def _mix_body(x_ref, oa_ref, g1_ref, w3_ref, bg_ref, gg_ref, wm_ref, bm_ref, wb_ref, wo_ref,
              g2_ref, y1_ref, h2t_ref, *gv_ref):
    x = x_ref[...]
    tt = x.shape[0]
    h = _rms(x, g1_ref[...]).astype(_BF)
    p = _dot(h, w3_ref[...])
    u = _gelu(p[:, :WIDTH])
    vn = _rms(_gelu(p[:, WIDTH:2 * WIDTH]), gg_ref[...])
    if gv_ref:
        gv_ref[0][...] = vn
    vnb = vn.astype(_BF)
    lane = lax.broadcasted_iota(jnp.int32, (MLP_CHUNK, LANES), 1)
    bias = bm_ref[...]
    chunks = []
    for c in range(tt // MLP_CHUNK):
        cols = []
        for m in range(WIDTH // LANES):
            vc = vnb[c * MLP_CHUNK:(c + 1) * MLP_CHUNK, m * LANES:(m + 1) * LANES]
            cols.append(jnp.where(lane < GROUP_DIM_MLP, _dot(wm_ref[2 * m], vc),
                                  _dot(wm_ref[2 * m + 1], vc)))
        chunks.append(jnp.concatenate(cols, axis=1) + bias)
    mixed = chunks[0] if len(chunks) == 1 else jnp.concatenate(chunks, axis=0)
    ob = (u * mixed).astype(_BF)
    gates = jax.nn.sigmoid(p[:, 2 * WIDTH:] + bg_ref[...])
    m = gates[:, :D_MODEL] * _dot(oa_ref[...], wb_ref[0]) + gates[:, D_MODEL:] * _dot(ob, wb_ref[1])
    y1 = x + _dot(m.astype(_BF), wo_ref[...])
    y1_ref[...] = y1
    h2t_ref[...] = _rms(y1, g2_ref[...]).T.astype(_BF)


def _mix(x, oa, g1, w3, bg, gg, wm, bm, wb, wo, g2, tt, emit_gv):
    t = x.shape[0]
    row = lambda i: (i, 0)
    fixed2 = lambda i: (0, 0)
    fixed3 = lambda i: (0, 0, 0)
    out_specs = [pl.BlockSpec((tt, D_MODEL), row), pl.BlockSpec((D_MODEL, tt), lambda i: (0, i))]
    out_shape = [jax.ShapeDtypeStruct((t, D_MODEL), _F32), jax.ShapeDtypeStruct((D_MODEL, t), _BF)]
    if emit_gv:
        out_specs.append(pl.BlockSpec((tt, WIDTH), row))
        out_shape.append(jax.ShapeDtypeStruct((t, WIDTH), _F32))
    return pl.pallas_call(
        _mix_body,
        grid=(t // tt,),
        in_specs=[pl.BlockSpec((tt, D_MODEL), row),
                  pl.BlockSpec((tt, WIDTH), row),
                  pl.BlockSpec((1, D_MODEL), fixed2),
                  pl.BlockSpec((D_MODEL, 2 * WIDTH + 2 * D_MODEL), fixed2),
                  pl.BlockSpec((1, 2 * D_MODEL), fixed2),
                  pl.BlockSpec((1, WIDTH), fixed2),
                  pl.BlockSpec((N_GROUPS_MLP, MLP_CHUNK, MLP_CHUNK), fixed3),
                  pl.BlockSpec((MLP_CHUNK, WIDTH), fixed2),
                  pl.BlockSpec((2, WIDTH, D_MODEL), fixed3),
                  pl.BlockSpec((D_MODEL, D_MODEL), fixed2),
                  pl.BlockSpec((1, D_MODEL), fixed2)],
        out_specs=out_specs,
        out_shape=out_shape,
        compiler_params=_params("parallel"),
        name="mixer_out",
    )(x, oa, g1, w3, bg, gg, wm, bm, wb, wo, g2)


def _top16(xs, exact):
    shape = xs[0].shape
    row = lax.broadcasted_iota(jnp.int32, shape, 0)
    ranks = [jnp.full(shape, N_KEYS - 1, jnp.int32) for _ in xs]
    vals = [[] for _ in xs]
    for a in range(PEER_TOPK):
        ms = [jnp.max(x, axis=0, keepdims=True) for x in xs]
        if exact:
            idxs = [jnp.min(jnp.where(x == m, row, N_KEYS), axis=0, keepdims=True)
                    for x, m in zip(xs, ms)]
            hits = [row == idx for idx in idxs]
        else:
            hits = [x == m for x, m in zip(xs, ms)]
        xs = [jnp.where(hit, _NEG_INF, x) for hit, x in zip(hits, xs)]
        ranks = [jnp.where(hit, a, rank) for hit, rank in zip(hits, ranks)]
        for v, m in zip(vals, ms):
            v.append(m)
    removed = [jnp.sum((x == _NEG_INF).astype(jnp.int32), axis=0, keepdims=True) for x in xs]
    return [jnp.concatenate(v, axis=0) for v in vals], ranks, removed


def _staircase(v1s, v2s):
    shape = v1s[0].shape
    arow = lax.broadcasted_iota(jnp.int32, shape, 0)
    counts = [jnp.zeros(shape, jnp.int32) for _ in v1s]
    for _ in range(PEER_TOPK):
        nxts = []
        for count, v2 in zip(counts, v2s):
            nxt = jnp.full(shape, _NEG_INF, _F32)
            for b in range(PEER_TOPK):
                nxt = jnp.where(count == b, v2[b:b + 1, :], nxt)
            nxts.append(nxt)
        cands = [v1 + nxt for v1, nxt in zip(v1s, nxts)]
        ms = [jnp.max(cand, axis=0, keepdims=True) for cand in cands]
        asels = [jnp.min(jnp.where(cand == m, arow, PEER_TOPK), axis=0, keepdims=True)
                 for cand, m in zip(cands, ms)]
        counts = [count + (arow == asel).astype(jnp.int32) for count, asel in zip(counts, asels)]
    return counts


def _peer_body(h2t_ref, y1_ref, wqt_ref, sk_ref, u_ref, vt_ref, gf_ref, y_ref,
               sc_ref, lr_ref, f1_ref, r2_ref, e2_ref, st_ref, wt_ref, acc_ref):
    c = pl.program_id(1)
    tt = h2t_ref.shape[1]
    nlg = tt // LANES
    ec = u_ref.shape[0]
    n1_per_chunk = ec // N_KEYS
    h2t = h2t_ref[...]

    @pl.when(c == 0)
    def _route():
        acc_ref[...] = jnp.zeros_like(acc_ref)
        kd = 2 * N_KEYS
        for h in range(PEER_HEADS):
            qt = _dot(wqt_ref[h * kd:(h + 1) * kd, :], h2t).astype(_BF)
            for half in range(2):
                st = _dot(sk_ref[h, half], qt[half * N_KEYS:(half + 1) * N_KEYS, :])
                for lg in range(nlg):
                    sc_ref[h, half, lg] = st[:, lg * LANES:(lg + 1) * LANES]

        def route_pair(idx, exact):
            slabs = [2 * idx, 2 * idx + 1]
            hs = [s // nlg for s in slabs]
            lgs = [s % nlg for s in slabs]
            s1s = [sc_ref[h, 0, lg] for h, lg in zip(hs, lgs)]
            s2s = [sc_ref[h, 1, lg] for h, lg in zip(hs, lgs)]
            vals, ranks, removed = _top16(s1s + s2s, exact)
            v1s, v2s = vals[:2], vals[2:]
            stairs = _staircase(v1s, v2s)
            for i in range(2):
                h, lg, v1, v2, stair = hs[i], lgs[i], v1s[i], v2s[i], stairs[i]
                e1s = jnp.exp(v1 - v1[0:1, :])
                e2s = jnp.exp(v2 - v2[0:1, :])
                pref = jnp.zeros_like(e2s)
                for b in range(PEER_TOPK):
                    pref = pref + jnp.where(stair > b, e2s[b:b + 1, :], 0.0)
                zsum = jnp.sum(e1s * pref, axis=0, keepdims=True)
                lr = jnp.zeros(s1s[i].shape, jnp.int32)
                for a in range(PEER_TOPK):
                    lr = jnp.where(ranks[i] == a, stair[a:a + 1, :], lr)
                blocked = lr_ref.shape[2:]
                lr_ref[h, lg] = lr.astype(_F32).reshape(blocked)
                f1_ref[h, lg] = (jnp.exp(s1s[i] - v1[0:1, :]) / zsum).reshape(blocked)
                r2_ref[h, lg] = ranks[2 + i].astype(_F32).astype(_BF)
                e2_ref[h, lg] = jnp.exp(s2s[i] - v2[0:1, :]).astype(_BF)
            return removed

        def route(idx, carry):
            removed = route_pair(idx, exact=False)
            tied = sum(jnp.sum((r != PEER_TOPK).astype(jnp.int32)) for r in removed)

            @pl.when(tied > 0)
            def _redo():
                route_pair(idx, exact=True)

            return carry

        lax.fori_loop(0, (PEER_HEADS * nlg) // 2, route, 0)

    es = min(ec, PEER_SUB_EXPERTS)
    for sub in range(ec // es):
        st_ref[sub * es:(sub + 1) * es, :] = _dot(u_ref[sub * es:(sub + 1) * es, :], h2t)
    for sub in range(ec // es):
        for k in range(es // N_KEYS):
            n1_local = sub * (es // N_KEYS) + k
            r0 = sub * es + k * N_KEYS
            for lg in range(nlg):
                gate = jnp.zeros((N_KEYS, LANES), _BF)
                for h in range(PEER_HEADS):
                    lrow = lr_ref[h, lg, c, n1_local:n1_local + 1, :].astype(_BF)
                    frow = f1_ref[h, lg, c, n1_local:n1_local + 1, :].astype(_BF)
                    gate = gate + jnp.where(r2_ref[h, lg] < lrow, e2_ref[h, lg] * frow,
                                            jnp.zeros((), _BF))
                act = _gelu(st_ref[r0:r0 + N_KEYS, lg * LANES:(lg + 1) * LANES])
                wt_ref[r0:r0 + N_KEYS, lg * LANES:(lg + 1) * LANES] = gate * act.astype(_BF)
        acc_ref[...] += _dot(vt_ref[:, sub * es:(sub + 1) * es], wt_ref[sub * es:(sub + 1) * es, :])

    @pl.when(c == pl.num_programs(1) - 1)
    def _finish():
        y_ref[...] = _rms(y1_ref[...] + acc_ref[...].T, gf_ref[...])


def _peer(h2t, y1, wqt, sk, u, vt, gf, tt):
    t = y1.shape[0]
    nlg = tt // LANES
    ec = PEER_STEP_EXPERTS
    assert N_EXPERTS % ec == 0 and (PEER_HEADS * nlg) % 2 == 0
    tile = lambda i, c: (i, 0)
    once = dict(pipeline_mode=pl.Buffered(1))
    slab = lambda dt: pltpu.VMEM((PEER_HEADS, nlg, N_KEYS, LANES), dt)
    rows = lambda: pltpu.VMEM((PEER_HEADS, nlg, N_EXPERTS // ec, ec // N_KEYS, LANES), _F32)
    return pl.pallas_call(
        _peer_body,
        grid=(t // tt, N_EXPERTS // ec),
        in_specs=[pl.BlockSpec((D_MODEL, tt), lambda i, c: (0, i)),
                  pl.BlockSpec((tt, D_MODEL), tile),
                  pl.BlockSpec((PEER_HEADS * 2 * N_KEYS, D_MODEL), lambda i, c: (0, 0), **once),
                  pl.BlockSpec((PEER_HEADS, 2, N_KEYS, N_KEYS), lambda i, c: (0, 0, 0, 0), **once),
                  pl.BlockSpec((ec, D_MODEL), lambda i, c: (c, 0)),
                  pl.BlockSpec((D_MODEL, ec), lambda i, c: (0, c)),
                  pl.BlockSpec((1, D_MODEL), lambda i, c: (0, 0))],
        out_specs=pl.BlockSpec((tt, D_MODEL), tile),
        out_shape=jax.ShapeDtypeStruct((t, D_MODEL), _F32),
        scratch_shapes=[pltpu.VMEM((PEER_HEADS, 2, nlg, N_KEYS, LANES), _F32),
                        rows(), rows(), slab(_BF), slab(_BF),
                        pltpu.VMEM((ec, tt), _F32),
                        pltpu.VMEM((ec, tt), _BF),
                        pltpu.VMEM((D_MODEL, tt), _F32)],
        compiler_params=_params("parallel", "arbitrary"),
        name="peer_ffn",
    )(h2t, y1, wqt, sk, u, vt, gf)


def _to_heads(t2d, b):
    t = t2d.shape[0] // b
    return t2d.reshape(b, t, N_HEADS_SB, HEAD_DIM_SB).transpose(0, 2, 1, 3)


def _split_streams(tokens_minor, b):
    bt = tokens_minor.shape[1]
    return tokens_minor.reshape(N_HEADS_SB, HEAD_DIM_SB, b, bt // b).transpose(2, 0, 3, 1)


def kernel(x_prompt, x_sample, cache_k, cache_v, norm_mix_g, w_in, b_gate, gmlp_norm_g, w_s, b_s,
           w_branch, w_out, norm_ffn_g, w_query, sub_keys, expert_u, expert_v, norm_final_g):
    assert w_in.shape[0] == 1, "single trunk layer"
    bp, sp, _ = x_prompt.shape
    bs, ss, _ = x_sample.shape
    assert bp == 1 and sp % MLP_CHUNK == 0 and bs * ss == LANES and ss <= CHUNK

    w_in_b = w_in[0].astype(_BF)
    w_qkv = w_in_b[:, :3 * WIDTH]
    w_rest = w_in_b[:, 3 * WIDTH:]
    g_mix = norm_mix_g[0][None, :]
    g_ffn = norm_ffn_g[0][None, :]
    g_fin = norm_final_g[None, :]
    g_mlp = gmlp_norm_g[0][None, :]
    bg = b_gate[0][None, :]
    wb = w_branch[0].astype(_BF)
    wo = w_out[0].astype(_BF)
    wqt = w_query[0].T.astype(_BF)
    sk = sub_keys[0].astype(_BF)
    u_b = expert_u[0].astype(_BF)
    vt_b = expert_v[0].T.astype(_BF)
    jj = lax.broadcasted_iota(jnp.int32, (KEY_BLOCK, 2 * KEY_BLOCK), 0)
    cc = lax.broadcasted_iota(jnp.int32, (KEY_BLOCK, 2 * KEY_BLOCK), 1)
    uo = ((cc >= KEY_BLOCK) | (jj >= cc)).astype(_BF)
    uo = jnp.concatenate([uo, uo], axis=0)

    pos = jnp.arange(MLP_CHUNK)
    causal = (pos[None, :] // CHUNK) <= (pos[:, None] // CHUNK)
    wm_p = jnp.where(causal[None], w_s[0], 0.0).astype(_BF)
    bm_p = jnp.repeat(b_s[0].T, GROUP_DIM_MLP, axis=1)
    same_stream = (pos[None, :] // ss) == (pos[:, None] // ss)
    wm_s = jnp.where(same_stream[None], jnp.tile(w_s[0][:, :ss, :ss], (1, bs, bs)), 0.0).astype(_BF)
    bm_s = jnp.repeat(jnp.tile(b_s[0][:, :ss], (1, bs)).T, GROUP_DIM_MLP, axis=1)

    xp = x_prompt[0]
    q_p, k_p, v_p, kb_p, vb_p = _qkv(xp, g_mix, w_qkv, 512)
    oa_p = _sb_prompt(q_p, kb_p, vb_p, uo)
    y1_p, h2t_p = _mix(xp, oa_p, g_mix, w_rest, bg, g_mlp, wm_p, bm_p, wb, wo, g_ffn, 512, False)
    y_p = _peer(h2t_p, y1_p, wqt, sk, u_b, vt_b, g_fin, 512)

    xs = x_sample.reshape(bs * ss, D_MODEL)
    q_s, k_s, v_s, kb_s, vb_s = _qkv(xs, g_mix, w_qkv, LANES)
    keys_minor = lambda a: a.transpose(0, 1, 3, 2)
    pad = ((0, 0), (0, 0), (0, 0), (0, KEY_BLOCK - ss))
    oa_s = _sb_sample(_to_heads(q_s, bs), jnp.pad(keys_minor(_to_heads(kb_s, bs)), pad),
                      jnp.pad(keys_minor(_to_heads(vb_s, bs)), pad),
                      keys_minor(cache_k[0]), keys_minor(cache_v[0]), uo)
    oa_s = oa_s.transpose(0, 2, 1, 3).reshape(bs * ss, WIDTH).astype(_BF)
    y1_s, h2t_s, gv_s = _mix(xs, oa_s, g_mix, w_rest, bg, g_mlp, wm_s, bm_s, wb, wo, g_ffn,
                             LANES, True)
    y_s = _peer(h2t_s, y1_s, wqt, sk, u_b, vt_b, g_fin, LANES)

    return (y_p[None],
            y_s.reshape(bs, ss, D_MODEL),
            _split_streams(k_p, bp)[None],
            _split_streams(v_p, bp)[None],
            _split_streams(k_s, bs)[None],
            _split_streams(v_s, bs)[None],
            gv_s.reshape(bs, ss, WIDTH)[None])
```

```python
import functools

import jax
import jax.numpy as jnp
from jax import lax
from jax.experimental import pallas as pl
from jax.experimental.pallas import tpu as pltpu

D_MODEL = 1024
N_HEADS_SB = 8
HEAD_DIM_SB = 64
WIDTH = N_HEADS_SB * HEAD_DIM_SB
N_GROUPS_MLP = 8
GROUP_DIM_MLP = WIDTH // N_GROUPS_MLP
MLP_CHUNK = 128
CHUNK = 64
PEER_HEADS = 8
N_KEYS = 128
N_EXPERTS = N_KEYS * N_KEYS
PEER_TOPK = 16
EPS = 1e-6
SB_SCALE = HEAD_DIM_SB ** -0.5

LANES = 128
KEY_BLOCK = LANES
Q_BLOCK = LANES
SB_Q_BLOCKS_PER_STEP = 2
SB_DEAD_LOGIT = -120.0
VMEM_LIMIT_BYTES = 60 * 1024 * 1024
PEER_STEP_EXPERTS = 2048
PEER_SUB_EXPERTS = 512

_BF = jnp.bfloat16
_F32 = jnp.float32
_NEG_INF = float("-inf")


def _rms(x, g):
    return x * lax.rsqrt(jnp.mean(x * x, axis=-1, keepdims=True) + EPS) * g


def _gelu(x):
    c = 0.7978845608028654
    half = 0.5 * x
    return half + half * jnp.tanh(x * (c + (c * 0.044715) * (x * x)))


def _dot(a, b):
    return jnp.dot(a, b, preferred_element_type=_F32)


def _params(*sem):
    return pltpu.CompilerParams(dimension_semantics=sem, vmem_limit_bytes=VMEM_LIMIT_BYTES)


def _qkv_body(x_ref, g_ref, w_ref, q_ref, k_ref, v_ref, kb_ref, vb_ref):
    h = _rms(x_ref[...], g_ref[...]).astype(_BF)
    p = _dot(h, w_ref[...])
    q_ref[...] = (p[:, :WIDTH] * SB_SCALE).astype(_BF)
    k = p[:, WIDTH:2 * WIDTH]
    v = p[:, 2 * WIDTH:]
    k_ref[...] = k.T
    v_ref[...] = v.T
    kb_ref[...] = k.astype(_BF)
    vb_ref[...] = v.astype(_BF)


def _qkv(x, g, w_qkv, tt):
    t = x.shape[0]
    row = lambda i: (i, 0)
    fixed = lambda i: (0, 0)
    heads = pl.BlockSpec((WIDTH, tt), lambda i: (0, i))
    return pl.pallas_call(
        _qkv_body,
        grid=(t // tt,),
        in_specs=[pl.BlockSpec((tt, D_MODEL), row),
                  pl.BlockSpec((1, D_MODEL), fixed),
                  pl.BlockSpec((D_MODEL, 3 * WIDTH), fixed)],
        out_specs=[pl.BlockSpec((tt, WIDTH), row), heads, heads,
                   pl.BlockSpec((tt, WIDTH), row), pl.BlockSpec((tt, WIDTH), row)],
        out_shape=[jax.ShapeDtypeStruct((t, WIDTH), _BF),
                   jax.ShapeDtypeStruct((WIDTH, t), _F32),
                   jax.ShapeDtypeStruct((WIDTH, t), _F32),
                   jax.ShapeDtypeStruct((t, WIDTH), _BF),
                   jax.ShapeDtypeStruct((t, WIDTH), _BF)],
        compiler_params=_params("parallel"),
        name="qkv_proj",
    )(x, g, w_qkv)


def _sb_blocks(qs, kbs, vbs, valids, rs, uo, keys_minor=False):
    if not isinstance(valids, (list, tuple)):
        valids = [valids] * len(qs)
    contract_last = (((1,), (1,)), ((), ()))
    if keys_minor:
        zs = [_dot(q, kb) for q, kb in zip(qs, kbs)]
    else:
        zs = [lax.dot_general(q, kb, contract_last, preferred_element_type=_F32)
              for q, kb in zip(qs, kbs)]
    hls = []
    for z, valid in zip(zs, valids):
        softplus = jnp.maximum(z, 0.0) + jnp.log1p(jnp.exp(-jnp.abs(z)))
        lom = jnp.where(valid, -softplus, 0.0)
        hi = lom.astype(_BF)
        lo = (lom - hi.astype(_F32)).astype(_BF)
        hls.append(jnp.concatenate([hi, lo], axis=1))
    css = [_dot(hl, uo) for hl in hls]
    ws = [jnp.where(valid, jnp.exp(z + cs[:, :KEY_BLOCK] + r), 0.0).astype(_BF)
          for z, cs, r, valid in zip(zs, css, rs, valids)]
    if keys_minor:
        pvs = [lax.dot_general(w, vb, contract_last, preferred_element_type=_F32)
               for w, vb in zip(ws, vbs)]
    else:
        pvs = [_dot(w, vb) for w, vb in zip(ws, vbs)]
    return pvs, [r + cs[:, KEY_BLOCK:] for r, cs in zip(rs, css)]


def _sb_prompt_body(q_ref, k_ref, v_ref, uo_ref, o_ref, q2_ref, acc_ref, r_ref):
    step = pl.program_id(0)
    n_pair = WIDTH // LANES
    nq = SB_Q_BLOCKS_PER_STEP
    chains = [(qb, p) for qb in range(nq) for p in range(n_pair)]
    lane = lax.broadcasted_iota(jnp.int32, (Q_BLOCK, LANES), 1)
    for ci, (qb, p) in enumerate(chains):
        q = q_ref[qb * Q_BLOCK:(qb + 1) * Q_BLOCK, p * LANES:(p + 1) * LANES]
        zero = jnp.zeros_like(q)
        q2_ref[ci] = jnp.concatenate([jnp.where(lane < HEAD_DIM_SB, q, zero),
                                      jnp.where(lane >= HEAD_DIM_SB, q, zero)], axis=0)
    acc_ref[...] = jnp.zeros_like(acc_ref)
    r_ref[...] = jnp.zeros_like(r_ref)
    row = lax.broadcasted_iota(jnp.int32, (2 * Q_BLOCK, KEY_BLOCK), 0) & (Q_BLOCK - 1)
    col = lax.broadcasted_iota(jnp.int32, (2 * Q_BLOCK, KEY_BLOCK), 1)
    last_block = step * nq + (nq - 1)

    def cond(c):
        d, rmax = c
        return jnp.logical_and(d <= last_block, rmax > SB_DEAD_LOGIT)

    def body(c):
        d, _ = c
        offs, valids = [], []
        for qb in range(nq):
            i = step * nq + qb
            j = i - d
            jc = jnp.maximum(j, 0)
            offs.append(pl.multiple_of(jc * KEY_BLOCK, KEY_BLOCK))
            valids.append(jnp.logical_and((col + jc * KEY_BLOCK) < (row + i * Q_BLOCK), j >= 0))
        pvs, rns = _sb_blocks(
            [q2_ref[ci] for ci in range(len(chains))],
            [k_ref[pl.ds(offs[qb], KEY_BLOCK), p * LANES:(p + 1) * LANES] for qb, p in chains],
            [v_ref[pl.ds(offs[qb], KEY_BLOCK), p * LANES:(p + 1) * LANES] for qb, p in chains],
            [valids[qb] for qb, _ in chains], [r_ref[ci] for ci in range(len(chains))], uo_ref[...])
        rmax = jnp.float32(_NEG_INF)
        for ci in range(len(chains)):
            acc_ref[ci] += pvs[ci]
            r_ref[ci] = rns[ci]
            rmax = jnp.maximum(rmax, jnp.max(rns[ci]))
        return d + 1, rmax

    lax.while_loop(cond, body, (jnp.int32(0), jnp.float32(0.0)))
    for ci, (qb, p) in enumerate(chains):
        acc = acc_ref[ci]
        o_ref[qb * Q_BLOCK:(qb + 1) * Q_BLOCK, p * LANES:(p + 1) * LANES] = jnp.where(
            lane < HEAD_DIM_SB, acc[:Q_BLOCK], acc[Q_BLOCK:]).astype(o_ref.dtype)


def _sb_prompt(q, kb, vb, uo):
    s = q.shape[0]
    n_chain = SB_Q_BLOCKS_PER_STEP * (WIDTH // LANES)
    rows = SB_Q_BLOCKS_PER_STEP * Q_BLOCK
    assert Q_BLOCK == KEY_BLOCK and s % rows == 0
    resident = lambda shape: pl.BlockSpec(shape, lambda i: (0, 0), pipeline_mode=pl.Buffered(1))
    return pl.pallas_call(
        _sb_prompt_body,
        grid=(s // rows,),
        in_specs=[pl.BlockSpec((rows, WIDTH), lambda i: (i, 0)),
                  resident((s, WIDTH)),
                  resident((s, WIDTH)),
                  resident((2 * KEY_BLOCK, 2 * KEY_BLOCK))],
        out_specs=pl.BlockSpec((rows, WIDTH), lambda i: (i, 0)),
        out_shape=jax.ShapeDtypeStruct((s, WIDTH), _BF),
        scratch_shapes=[pltpu.VMEM((n_chain, 2 * Q_BLOCK, LANES), _BF),
                        pltpu.VMEM((n_chain, 2 * Q_BLOCK, LANES), _F32),
                        pltpu.VMEM((n_chain, 2 * Q_BLOCK, LANES), _F32)],
        compiler_params=_params("parallel"),
        name="sb_prompt",
    )(q, kb, vb, uo)


def _sb_sample_body(q_ref, kn_ref, vn_ref, kp_ref, vp_ref, uo_ref, o_ref, acc_ref, r_ref):
    n_head, tq, _ = q_ref.shape
    heads = range(n_head)
    qs = [q_ref[hd] for hd in heads]
    row = lax.broadcasted_iota(jnp.int32, (tq, KEY_BLOCK), 0)
    col = lax.broadcasted_iota(jnp.int32, (tq, KEY_BLOCK), 1)
    pvs, r0s = _sb_blocks(qs, [kn_ref[hd] for hd in heads], [vn_ref[hd] for hd in heads], col < row,
                          [jnp.zeros((tq, LANES), _F32) for _ in heads], uo_ref[...],
                          keys_minor=True)
    rmax0 = jnp.float32(_NEG_INF)
    for hd in heads:
        acc_ref[hd] = pvs[hd]
        r_ref[hd] = r0s[hd]
        rmax0 = jnp.maximum(rmax0, jnp.max(r0s[hd]))
    always = col >= 0
    n_past = kp_ref.shape[2] // KEY_BLOCK

    def cond(c):
        j, rmax = c
        return jnp.logical_and(j >= 0, rmax > SB_DEAD_LOGIT)

    def body(c):
        j, _ = c
        off = pl.multiple_of(j * KEY_BLOCK, KEY_BLOCK)
        pvs, rns = _sb_blocks(qs, [kp_ref[hd, :, pl.ds(off, KEY_BLOCK)].astype(_BF) for hd in heads],
                              [vp_ref[hd, :, pl.ds(off, KEY_BLOCK)].astype(_BF) for hd in heads],
                              always, [r_ref[hd] for hd in heads], uo_ref[...], keys_minor=True)
        rmax = jnp.float32(_NEG_INF)
        for hd in heads:
            acc_ref[hd] += pvs[hd]
            r_ref[hd] = rns[hd]
            rmax = jnp.maximum(rmax, jnp.max(rns[hd]))
        return j - 1, rmax

    lax.while_loop(cond, body, (jnp.int32(n_past - 1), rmax0))
    o_ref[...] = acc_ref[...]


def _sb_sample(q, kn, vn, kp, vp, uo):
    b, h, tq, dh = q.shape
    past = kp.shape[3]
    blk = lambda n: pl.BlockSpec((None, h, n, dh), lambda bi: (bi, 0, 0, 0))
    blk_t = lambda n: pl.BlockSpec((None, h, dh, n), lambda bi: (bi, 0, 0, 0))
    return pl.pallas_call(
        _sb_sample_body,
        grid=(b,),
        in_specs=[blk(tq), blk_t(KEY_BLOCK), blk_t(KEY_BLOCK), blk_t(past), blk_t(past),
                  pl.BlockSpec((2 * KEY_BLOCK, 2 * KEY_BLOCK), lambda bi: (0, 0))],
        out_specs=blk(tq),
        out_shape=jax.ShapeDtypeStruct((b, h, tq, dh), _F32),
        scratch_shapes=[pltpu.VMEM((h, tq, dh), _F32), pltpu.VMEM((h, tq, LANES), _F32)],
        compiler_params=_params("parallel"),
        name="sb_sample",
    )(q, kn, vn, kp, vp, uo)


def _mix_body(x_ref, oa_ref, g1_ref, w3_ref, bg_ref, gg_ref, wm_ref, bm_ref, wb_ref, wo_ref,
              g2_ref, y1_ref, h2t_ref, *gv_ref):
    x = x_ref[...]
    tt = x.shape[0]
    h = _rms(x, g1_ref[...]).astype(_BF)
    p = _dot(h, w3_ref[...])
    u = _gelu(p[:, :WIDTH])
    vn = _rms(_gelu(p[:, WIDTH:2 * WIDTH]), gg_ref[...])
    if gv_ref:
        gv_ref[0][...] = vn
    vnb = vn.astype(_BF)
    lane = lax.broadcasted_iota(jnp.int32, (MLP_CHUNK, LANES), 1)
    bias = bm_ref[...]
    chunks = []
    for c in range(tt // MLP_CHUNK):
        cols = []
        for m in range(WIDTH // LANES):
            vc = vnb[c * MLP_CHUNK:(c + 1) * MLP_CHUNK, m * LANES:(m + 1) * LANES]
            cols.append(jnp.where(lane < GROUP_DIM_MLP, _dot(wm_ref[2 * m], vc),
                                  _dot(wm_ref[2 * m + 1], vc)))
        chunks.append(jnp.concatenate(cols, axis=1) + bias)
    mixed = chunks[0] if len(chunks) == 1 else jnp.concatenate(chunks, axis=0)
    ob = (u * mixed).astype(_BF)
    gates = jax.nn.sigmoid(p[:, 2 * WIDTH:] + bg_ref[...])
    m = gates[:, :D_MODEL] * _dot(oa_ref[...], wb_ref[0]) + gates[:, D_MODEL:] * _dot(ob, wb_ref[1])
    y1 = x + _dot(m.astype(_BF), wo_ref[...])
    y1_ref[...] = y1
    h2t_ref[...] = _rms(y1, g2_ref[...]).T.astype(_BF)


def _mix(x, oa, g1, w3, bg, gg, wm, bm, wb, wo, g2, tt, emit_gv):
    t = x.shape[0]
    row = lambda i: (i, 0)
    fixed2 = lambda i: (0, 0)
    fixed3 = lambda i: (0, 0, 0)
    out_specs = [pl.BlockSpec((tt, D_MODEL), row), pl.BlockSpec((D_MODEL, tt), lambda i: (0, i))]
    out_shape = [jax.ShapeDtypeStruct((t, D_MODEL), _F32), jax.ShapeDtypeStruct((D_MODEL, t), _BF)]
    if emit_gv:
        out_specs.append(pl.BlockSpec((tt, WIDTH), row))
        out_shape.append(jax.ShapeDtypeStruct((t, WIDTH), _F32))
    return pl.pallas_call(
        _mix_body,
        grid=(t // tt,),
        in_specs=[pl.BlockSpec((tt, D_MODEL), row),
                  pl.BlockSpec((tt, WIDTH), row),
                  pl.BlockSpec((1, D_MODEL), fixed2),
                  pl.BlockSpec((D_MODEL, 2 * WIDTH + 2 * D_MODEL), fixed2),
                  pl.BlockSpec((1, 2 * D_MODEL), fixed2),
                  pl.BlockSpec((1, WIDTH), fixed2),
                  pl.BlockSpec((N_GROUPS_MLP, MLP_CHUNK, MLP_CHUNK), fixed3),
                  pl.BlockSpec((MLP_CHUNK, WIDTH), fixed2),
                  pl.BlockSpec((2, WIDTH, D_MODEL), fixed3),
                  pl.BlockSpec((D_MODEL, D_MODEL), fixed2),
                  pl.BlockSpec((1, D_MODEL), fixed2)],
        out_specs=out_specs,
        out_shape=out_shape,
        compiler_params=_params("parallel"),
        name="mixer_out",
    )(x, oa, g1, w3, bg, gg, wm, bm, wb, wo, g2)


def _top16(xs, exact):
    shape = xs[0].shape
    row = lax.broadcasted_iota(jnp.int32, shape, 0)
    ranks = [jnp.full(shape, N_KEYS - 1, jnp.int32) for _ in xs]
    vals = [[] for _ in xs]
    for a in range(PEER_TOPK):
        ms = [jnp.max(x, axis=0, keepdims=True) for x in xs]
        if exact:
            idxs = [jnp.min(jnp.where(x == m, row, N_KEYS), axis=0, keepdims=True)
                    for x, m in zip(xs, ms)]
            hits = [row == idx for idx in idxs]
        else:
            hits = [x == m for x, m in zip(xs, ms)]
        xs = [jnp.where(hit, _NEG_INF, x) for hit, x in zip(hits, xs)]
        ranks = [jnp.where(hit, a, rank) for hit, rank in zip(hits, ranks)]
        for v, m in zip(vals, ms):
            v.append(m)
    removed = [jnp.sum((x == _NEG_INF).astype(jnp.int32), axis=0, keepdims=True) for x in xs]
    return [jnp.concatenate(v, axis=0) for v in vals], ranks, removed


def _staircase(v1s, v2s):
    shape = v1s[0].shape
    arow = lax.broadcasted_iota(jnp.int32, shape, 0)
    counts = [jnp.zeros(shape, jnp.int32) for _ in v1s]
    for _ in range(PEER_TOPK):
        nxts = []
        for count, v2 in zip(counts, v2s):
            nxt = jnp.full(shape, _NEG_INF, _F32)
            for b in range(PEER_TOPK):
                nxt = jnp.where(count == b, v2[b:b + 1, :], nxt)
            nxts.append(nxt)
        cands = [v1 + nxt for v1, nxt in zip(v1s, nxts)]
        ms = [jnp.max(cand, axis=0, keepdims=True) for cand in cands]
        asels = [jnp.min(jnp.where(cand == m, arow, PEER_TOPK), axis=0, keepdims=True)
                 for cand, m in zip(cands, ms)]
        counts = [count + (arow == asel).astype(jnp.int32) for count, asel in zip(counts, asels)]
    return counts


def _peer_body(h2t_ref, y1_ref, wqt_ref, sk_ref, u_ref, vt_ref, gf_ref, y_ref,
               sc_ref, lr_ref, f1_ref, r2_ref, e2_ref, st_ref, wt_ref, acc_ref):
    c = pl.program_id(1)
    tt = h2t_ref.shape[1]
    nlg = tt // LANES
    ec = u_ref.shape[0]
    n1_per_chunk = ec // N_KEYS
    h2t = h2t_ref[...]

    @pl.when(c == 0)
    def _route():
        acc_ref[...] = jnp.zeros_like(acc_ref)
        kd = 2 * N_KEYS
        for h in range(PEER_HEADS):
            qt = _dot(wqt_ref[h * kd:(h + 1) * kd, :], h2t).astype(_BF)
            for half in range(2):
                st = _dot(sk_ref[h, half], qt[half * N_KEYS:(half + 1) * N_KEYS, :])
                for lg in range(nlg):
                    sc_ref[h, half, lg] = st[:, lg * LANES:(lg + 1) * LANES]

        def route_pair(idx, exact):
            slabs = [2 * idx, 2 * idx + 1]
            hs = [s // nlg for s in slabs]
            lgs = [s % nlg for s in slabs]
            s1s = [sc_ref[h, 0, lg] for h, lg in zip(hs, lgs)]
            s2s = [sc_ref[h, 1, lg] for h, lg in zip(hs, lgs)]
            vals, ranks, removed = _top16(s1s + s2s, exact)
            v1s, v2s = vals[:2], vals[2:]
            stairs = _staircase(v1s, v2s)
            for i in range(2):
                h, lg, v1, v2, stair = hs[i], lgs[i], v1s[i], v2s[i], stairs[i]
                e1s = jnp.exp(v1 - v1[0:1, :])
                e2s = jnp.exp(v2 - v2[0:1, :])
                pref = jnp.zeros_like(e2s)
                for b in range(PEER_TOPK):
                    pref = pref + jnp.where(stair > b, e2s[b:b + 1, :], 0.0)
                zsum = jnp.sum(e1s * pref, axis=0, keepdims=True)
                lr = jnp.zeros(s1s[i].shape, jnp.int32)
                for a in range(PEER_TOPK):
                    lr = jnp.where(ranks[i] == a, stair[a:a + 1, :], lr)
                blocked = lr_ref.shape[2:]
                lr_ref[h, lg] = lr.astype(_F32).reshape(blocked)
                f1_ref[h, lg] = (jnp.exp(s1s[i] - v1[0:1, :]) / zsum).reshape(blocked)
                r2_ref[h, lg] = ranks[2 + i].astype(_F32).astype(_BF)
                e2_ref[h, lg] = jnp.exp(s2s[i] - v2[0:1, :]).astype(_BF)
            return removed

        def route(idx, carry):
            removed = route_pair(idx, exact=False)
            tied = sum(jnp.sum((r != PEER_TOPK).astype(jnp.int32)) for r in removed)

            @pl.when(tied > 0)
            def _redo():
                route_pair(idx, exact=True)

            return carry

        lax.fori_loop(0, (PEER_HEADS * nlg) // 2, route, 0)

    es = min(ec, PEER_SUB_EXPERTS)
    for sub in range(ec // es):
        st_ref[sub * es:(sub + 1) * es, :] = _dot(u_ref[sub * es:(sub + 1) * es, :], h2t)
    for sub in range(ec // es):
        for k in range(es // N_KEYS):
            n1_local = sub * (es // N_KEYS) + k
            r0 = sub * es + k * N_KEYS
            for lg in range(nlg):
                gate = jnp.zeros((N_KEYS, LANES), _BF)
                for h in range(PEER_HEADS):
                    lrow = lr_ref[h, lg, c, n1_local:n1_local + 1, :].astype(_BF)
                    frow = f1_ref[h, lg, c, n1_local:n1_local + 1, :].astype(_BF)
                    gate = gate + jnp.where(r2_ref[h, lg] < lrow, e2_ref[h, lg] * frow,
                                            jnp.zeros((), _BF))
                act = _gelu(st_ref[r0:r0 + N_KEYS, lg * LANES:(lg + 1) * LANES])
                wt_ref[r0:r0 + N_KEYS, lg * LANES:(lg + 1) * LANES] = gate * act.astype(_BF)
        acc_ref[...] += _dot(vt_ref[:, sub * es:(sub + 1) * es], wt_ref[sub * es:(sub + 1) * es, :])

    @pl.when(c == pl.num_programs(1) - 1)
    def _finish():
        y_ref[...] = _rms(y1_ref[...] + acc_ref[...].T, gf_ref[...])


def _peer(h2t, y1, wqt, sk, u, vt, gf, tt):
    t = y1.shape[0]
    nlg = tt // LANES
    ec = PEER_STEP_EXPERTS
    assert N_EXPERTS % ec == 0 and (PEER_HEADS * nlg) % 2 == 0
    tile = lambda i, c: (i, 0)
    once = dict(pipeline_mode=pl.Buffered(1))
    slab = lambda dt: pltpu.VMEM((PEER_HEADS, nlg, N_KEYS, LANES), dt)
    rows = lambda: pltpu.VMEM((PEER_HEADS, nlg, N_EXPERTS // ec, ec // N_KEYS, LANES), _F32)
    return pl.pallas_call(
        _peer_body,
        grid=(t // tt, N_EXPERTS // ec),
        in_specs=[pl.BlockSpec((D_MODEL, tt), lambda i, c: (0, i)),
                  pl.BlockSpec((tt, D_MODEL), tile),
                  pl.BlockSpec((PEER_HEADS * 2 * N_KEYS, D_MODEL), lambda i, c: (0, 0), **once),
                  pl.BlockSpec((PEER_HEADS, 2, N_KEYS, N_KEYS), lambda i, c: (0, 0, 0, 0), **once),
                  pl.BlockSpec((ec, D_MODEL), lambda i, c: (c, 0)),
                  pl.BlockSpec((D_MODEL, ec), lambda i, c: (0, c)),
                  pl.BlockSpec((1, D_MODEL), lambda i, c: (0, 0))],
        out_specs=pl.BlockSpec((tt, D_MODEL), tile),
        out_shape=jax.ShapeDtypeStruct((t, D_MODEL), _F32),
        scratch_shapes=[pltpu.VMEM((PEER_HEADS, 2, nlg, N_KEYS, LANES), _F32),
                        rows(), rows(), slab(_BF), slab(_BF),
                        pltpu.VMEM((ec, tt), _F32),
                        pltpu.VMEM((ec, tt), _BF),
                        pltpu.VMEM((D_MODEL, tt), _F32)],
        compiler_params=_params("parallel", "arbitrary"),
        name="peer_ffn",
    )(h2t, y1, wqt, sk, u, vt, gf)


def _to_heads(t2d, b):
    t = t2d.shape[0] // b
    return t2d.reshape(b, t, N_HEADS_SB, HEAD_DIM_SB).transpose(0, 2, 1, 3)


def _split_streams(tokens_minor, b):
    bt = tokens_minor.shape[1]
    return tokens_minor.reshape(N_HEADS_SB, HEAD_DIM_SB, b, bt // b).transpose(2, 0, 3, 1)


def kernel(x_prompt, x_sample, cache_k, cache_v, norm_mix_g, w_in, b_gate, gmlp_norm_g, w_s, b_s,
           w_branch, w_out, norm_ffn_g, w_query, sub_keys, expert_u, expert_v, norm_final_g):
    assert w_in.shape[0] == 1, "single trunk layer"
    bp, sp, _ = x_prompt.shape
    bs, ss, _ = x_sample.shape
    assert bp == 1 and sp % MLP_CHUNK == 0 and bs * ss == LANES and ss <= CHUNK

    w_in_b = w_in[0].astype(_BF)
    w_qkv = w_in_b[:, :3 * WIDTH]
    w_rest = w_in_b[:, 3 * WIDTH:]
    g_mix = norm_mix_g[0][None, :]
    g_ffn = norm_ffn_g[0][None, :]
    g_fin = norm_final_g[None, :]
    g_mlp = gmlp_norm_g[0][None, :]
    bg = b_gate[0][None, :]
    wb = w_branch[0].astype(_BF)
    wo = w_out[0].astype(_BF)
    wqt = w_query[0].T.astype(_BF)
    sk = sub_keys[0].astype(_BF)
    u_b = expert_u[0].astype(_BF)
    vt_b = expert_v[0].T.astype(_BF)
    jj = lax.broadcasted_iota(jnp.int32, (KEY_BLOCK, 2 * KEY_BLOCK), 0)
    cc = lax.broadcasted_iota(jnp.int32, (KEY_BLOCK, 2 * KEY_BLOCK), 1)
    uo = ((cc >= KEY_BLOCK) | (jj >= cc)).astype(_BF)
    uo = jnp.concatenate([uo, uo], axis=0)

    pos = jnp.arange(MLP_CHUNK)
    causal = (pos[None, :] // CHUNK) <= (pos[:, None] // CHUNK)
    wm_p = jnp.where(causal[None], w_s[0], 0.0).astype(_BF)
    bm_p = jnp.repeat(b_s[0].T, GROUP_DIM_MLP, axis=1)
    same_stream = (pos[None, :] // ss) == (pos[:, None] // ss)
    wm_s = jnp.where(same_stream[None], jnp.tile(w_s[0][:, :ss, :ss], (1, bs, bs)), 0.0).astype(_BF)
    bm_s = jnp.repeat(jnp.tile(b_s[0][:, :ss], (1, bs)).T, GROUP_DIM_MLP, axis=1)

    xp = x_prompt[0]
    q_p, k_p, v_p, kb_p, vb_p = _qkv(xp, g_mix, w_qkv, 512)
    oa_p = _sb_prompt(q_p, kb_p, vb_p, uo)
    y1_p, h2t_p = _mix(xp, oa_p, g_mix, w_rest, bg, g_mlp, wm_p, bm_p, wb, wo, g_ffn, 512, False)
    y_p = _peer(h2t_p, y1_p, wqt, sk, u_b, vt_b, g_fin, 512)

    xs = x_sample.reshape(bs * ss, D_MODEL)
    q_s, k_s, v_s, kb_s, vb_s = _qkv(xs, g_mix, w_qkv, LANES)
    keys_minor = lambda a: a.transpose(0, 1, 3, 2)
    pad = ((0, 0), (0, 0), (0, 0), (0, KEY_BLOCK - ss))
    oa_s = _sb_sample(_to_heads(q_s, bs), jnp.pad(keys_minor(_to_heads(kb_s, bs)), pad),
                      jnp.pad(keys_minor(_to_heads(vb_s, bs)), pad),
                      keys_minor(cache_k[0]), keys_minor(cache_v[0]), uo)
    oa_s = oa_s.transpose(0, 2, 1, 3).reshape(bs * ss, WIDTH).astype(_BF)
    y1_s, h2t_s, gv_s = _mix(xs, oa_s, g_mix, w_rest, bg, g_mlp, wm_s, bm_s, wb, wo, g_ffn,
                             LANES, True)
    y_s = _peer(h2t_s, y1_s, wqt, sk, u_b, vt_b, g_fin, LANES)

    return (y_p[None],
            y_s.reshape(bs, ss, D_MODEL),
            _split_streams(k_p, bp)[None],
            _split_streams(v_p, bp)[None],
            _split_streams(k_s, bs)[None],
            _split_streams(v_s, bs)[None],
            gv_s.reshape(bs, ss, WIDTH)[None])
```

```python
import functools

import jax
import jax.numpy as jnp
from jax import lax
from jax.experimental import pallas as pl
from jax.experimental.pallas import tpu as pltpu

D_MODEL = 1024
N_HEADS_SB = 8
HEAD_DIM_SB = 64
WIDTH = N_HEADS_SB * HEAD_DIM_SB
N_GROUPS_MLP = 8
GROUP_DIM_MLP = WIDTH // N_GROUPS_MLP
MLP_CHUNK = 128
CHUNK = 64
PEER_HEADS = 8
N_KEYS = 128
N_EXPERTS = N_KEYS * N_KEYS
PEER_TOPK = 16
EPS = 1e-6
SB_SCALE = HEAD_DIM_SB ** -0.5

LANES = 128
KEY_BLOCK = LANES
Q_BLOCK = LANES
SB_Q_BLOCKS_PER_STEP = 4
SB_DEAD_LOGIT = -120.0
VMEM_LIMIT_BYTES = 60 * 1024 * 1024
PEER_STEP_EXPERTS = 2048
PEER_SUB_EXPERTS = 512

_BF = jnp.bfloat16
_F32 = jnp.float32
_NEG_INF = float("-inf")


def _rms(x, g):
    return x * lax.rsqrt(jnp.mean(x * x, axis=-1, keepdims=True) + EPS) * g


def _gelu(x):
    c = 0.7978845608028654
    half = 0.5 * x
    return half + half * jnp.tanh(x * (c + (c * 0.044715) * (x * x)))


def _dot(a, b):
    return jnp.dot(a, b, preferred_element_type=_F32)


def _params(*sem):
    return pltpu.CompilerParams(dimension_semantics=sem, vmem_limit_bytes=VMEM_LIMIT_BYTES)


def _qkv_body(x_ref, g_ref, w_ref, q_ref, k_ref, v_ref, kb_ref, vb_ref):
    h = _rms(x_ref[...], g_ref[...]).astype(_BF)
    p = _dot(h, w_ref[...])
    q_ref[...] = (p[:, :WIDTH] * SB_SCALE).astype(_BF)
    k = p[:, WIDTH:2 * WIDTH]
    v = p[:, 2 * WIDTH:]
    k_ref[...] = k.T
    v_ref[...] = v.T
    kb_ref[...] = k.astype(_BF)
    vb_ref[...] = v.astype(_BF)


def _qkv(x, g, w_qkv, tt):
    t = x.shape[0]
    row = lambda i: (i, 0)
    fixed = lambda i: (0, 0)
    heads = pl.BlockSpec((WIDTH, tt), lambda i: (0, i))
    return pl.pallas_call(
        _qkv_body,
        grid=(t // tt,),
        in_specs=[pl.BlockSpec((tt, D_MODEL), row),
                  pl.BlockSpec((1, D_MODEL), fixed),
                  pl.BlockSpec((D_MODEL, 3 * WIDTH), fixed)],
        out_specs=[pl.BlockSpec((tt, WIDTH), row), heads, heads,
                   pl.BlockSpec((tt, WIDTH), row), pl.BlockSpec((tt, WIDTH), row)],
        out_shape=[jax.ShapeDtypeStruct((t, WIDTH), _BF),
                   jax.ShapeDtypeStruct((WIDTH, t), _F32),
                   jax.ShapeDtypeStruct((WIDTH, t), _F32),
                   jax.ShapeDtypeStruct((t, WIDTH), _BF),
                   jax.ShapeDtypeStruct((t, WIDTH), _BF)],
        compiler_params=_params("parallel"),
        name="qkv_proj",
    )(x, g, w_qkv)


def _sb_blocks(qs, kbs, vbs, valids, rs, uo, keys_minor=False):
    if not isinstance(valids, (list, tuple)):
        valids = [valids] * len(qs)
    contract_last = (((1,), (1,)), ((), ()))
    if keys_minor:
        zs = [_dot(q, kb) for q, kb in zip(qs, kbs)]
    else:
        zs = [lax.dot_general(q, kb, contract_last, preferred_element_type=_F32)
              for q, kb in zip(qs, kbs)]
    hls = []
    for z, valid in zip(zs, valids):
        softplus = jnp.maximum(z, 0.0) + jnp.log1p(jnp.exp(-jnp.abs(z)))
        lom = jnp.where(valid, -softplus, 0.0)
        hi = lom.astype(_BF)
        lo = (lom - hi.astype(_F32)).astype(_BF)
        hls.append(jnp.concatenate([hi, lo], axis=1))
    css = [_dot(hl, uo) for hl in hls]
    ws = [jnp.where(valid, jnp.exp(z + cs[:, :KEY_BLOCK] + r), 0.0).astype(_BF)
          for z, cs, r, valid in zip(zs, css, rs, valids)]
    if keys_minor:
        pvs = [lax.dot_general(w, vb, contract_last, preferred_element_type=_F32)
               for w, vb in zip(ws, vbs)]
    else:
        pvs = [_dot(w, vb) for w, vb in zip(ws, vbs)]
    return pvs, [r + cs[:, KEY_BLOCK:] for r, cs in zip(rs, css)]


def _sb_prompt_body(q_ref, k_ref, v_ref, uo_ref, o_ref, q2_ref, acc_ref, r_ref):
    step = pl.program_id(0)
    n_pair = WIDTH // LANES
    nq = SB_Q_BLOCKS_PER_STEP
    chains = [(qb, p) for qb in range(nq) for p in range(n_pair)]
    lane = lax.broadcasted_iota(jnp.int32, (Q_BLOCK, LANES), 1)
    for ci, (qb, p) in enumerate(chains):
        q = q_ref[qb * Q_BLOCK:(qb + 1) * Q_BLOCK, p * LANES:(p + 1) * LANES]
        zero = jnp.zeros_like(q)
        q2_ref[ci] = jnp.concatenate([jnp.where(lane < HEAD_DIM_SB, q, zero),
                                      jnp.where(lane >= HEAD_DIM_SB, q, zero)], axis=0)
    acc_ref[...] = jnp.zeros_like(acc_ref)
    r_ref[...] = jnp.zeros_like(r_ref)
    row = lax.broadcasted_iota(jnp.int32, (2 * Q_BLOCK, KEY_BLOCK), 0) & (Q_BLOCK - 1)
    col = lax.broadcasted_iota(jnp.int32, (2 * Q_BLOCK, KEY_BLOCK), 1)
    last_block = step * nq + (nq - 1)

    def cond(c):
        d, rmax = c
        return jnp.logical_and(d <= last_block, rmax > SB_DEAD_LOGIT)

    def body(c):
        d, _ = c
        offs, valids = [], []
        for qb in range(nq):
            i = step * nq + qb
            j = i - d
            jc = jnp.maximum(j, 0)
            offs.append(pl.multiple_of(jc * KEY_BLOCK, KEY_BLOCK))
            valids.append(jnp.logical_and((col + jc * KEY_BLOCK) < (row + i * Q_BLOCK), j >= 0))
        pvs, rns = _sb_blocks(
            [q2_ref[ci] for ci in range(len(chains))],
            [k_ref[pl.ds(offs[qb], KEY_BLOCK), p * LANES:(p + 1) * LANES] for qb, p in chains],
            [v_ref[pl.ds(offs[qb], KEY_BLOCK), p * LANES:(p + 1) * LANES] for qb, p in chains],
            [valids[qb] for qb, _ in chains], [r_ref[ci] for ci in range(len(chains))], uo_ref[...])
        rmax = jnp.float32(_NEG_INF)
        for ci in range(len(chains)):
            acc_ref[ci] += pvs[ci]
            r_ref[ci] = rns[ci]
            rmax = jnp.maximum(rmax, jnp.max(rns[ci]))
        return d + 1, rmax

    lax.while_loop(cond, body, (jnp.int32(0), jnp.float32(0.0)))
    for ci, (qb, p) in enumerate(chains):
        acc = acc_ref[ci]
        o_ref[qb * Q_BLOCK:(qb + 1) * Q_BLOCK, p * LANES:(p + 1) * LANES] = jnp.where(
            lane < HEAD_DIM_SB, acc[:Q_BLOCK], acc[Q_BLOCK:]).astype(o_ref.dtype)


def _sb_prompt(q, kb, vb, uo):
    s = q.shape[0]
    n_chain = SB_Q_BLOCKS_PER_STEP * (WIDTH // LANES)
    rows = SB_Q_BLOCKS_PER_STEP * Q_BLOCK
    assert Q_BLOCK == KEY_BLOCK and s % rows == 0
    resident = lambda shape: pl.BlockSpec(shape, lambda i: (0, 0), pipeline_mode=pl.Buffered(1))
    return pl.pallas_call(
        _sb_prompt_body,
        grid=(s // rows,),
        in_specs=[pl.BlockSpec((rows, WIDTH), lambda i: (i, 0)),
                  resident((s, WIDTH)),
                  resident((s, WIDTH)),
                  resident((2 * KEY_BLOCK, 2 * KEY_BLOCK))],
        out_specs=pl.BlockSpec((rows, WIDTH), lambda i: (i, 0)),
        out_shape=jax.ShapeDtypeStruct((s, WIDTH), _BF),
        scratch_shapes=[pltpu.VMEM((n_chain, 2 * Q_BLOCK, LANES), _BF),
                        pltpu.VMEM((n_chain, 2 * Q_BLOCK, LANES), _F32),
                        pltpu.VMEM((n_chain, 2 * Q_BLOCK, LANES), _F32)],
        compiler_params=_params("parallel"),
        name="sb_prompt",
    )(q, kb, vb, uo)


def _sb_sample_body(q_ref, kn_ref, vn_ref, kp_ref, vp_ref, uo_ref, o_ref, acc_ref, r_ref):
    n_head, tq, _ = q_ref.shape
    heads = range(n_head)
    qs = [q_ref[hd] for hd in heads]
    row = lax.broadcasted_iota(jnp.int32, (tq, KEY_BLOCK), 0)
    col = lax.broadcasted_iota(jnp.int32, (tq, KEY_BLOCK), 1)
    pvs, r0s = _sb_blocks(qs, [kn_ref[hd] for hd in heads], [vn_ref[hd] for hd in heads], col < row,
                          [jnp.zeros((tq, LANES), _F32) for _ in heads], uo_ref[...],
                          keys_minor=True)
    rmax0 = jnp.float32(_NEG_INF)
    for hd in heads:
        acc_ref[hd] = pvs[hd]
        r_ref[hd] = r0s[hd]
        rmax0 = jnp.maximum(rmax0, jnp.max(r0s[hd]))
    always = col >= 0
    n_past = kp_ref.shape[2] // KEY_BLOCK

    def cond(c):
        j, rmax = c
        return jnp.logical_and(j >= 0, rmax > SB_DEAD_LOGIT)

    def body(c):
        j, _ = c
        off = pl.multiple_of(j * KEY_BLOCK, KEY_BLOCK)
        pvs, rns = _sb_blocks(qs, [kp_ref[hd, :, pl.ds(off, KEY_BLOCK)].astype(_BF) for hd in heads],
                              [vp_ref[hd, :, pl.ds(off, KEY_BLOCK)].astype(_BF) for hd in heads],
                              always, [r_ref[hd] for hd in heads], uo_ref[...], keys_minor=True)
        rmax = jnp.float32(_NEG_INF)
        for hd in heads:
            acc_ref[hd] += pvs[hd]
            r_ref[hd] = rns[hd]
            rmax = jnp.maximum(rmax, jnp.max(rns[hd]))
        return j - 1, rmax

    lax.while_loop(cond, body, (jnp.int32(n_past - 1), rmax0))
    o_ref[...] = acc_ref[...]


def _sb_sample(q, kn, vn, kp, vp, uo):
    b, h, tq, dh = q.shape
    past = kp.shape[3]
    blk = lambda n: pl.BlockSpec((None, h, n, dh), lambda bi: (bi, 0, 0, 0))
    blk_t = lambda n: pl.BlockSpec((None, h, dh, n), lambda bi: (bi, 0, 0, 0))
    return pl.pallas_call(
        _sb_sample_body,
        grid=(b,),
        in_specs=[blk(tq), blk_t(KEY_BLOCK), blk_t(KEY_BLOCK), blk_t(past), blk_t(past),
                  pl.BlockSpec((2 * KEY_BLOCK, 2 * KEY_BLOCK), lambda bi: (0, 0))],
        out_specs=blk(tq),
        out_shape=jax.ShapeDtypeStruct((b, h, tq, dh), _F32),
        scratch_shapes=[pltpu.VMEM((h, tq, dh), _F32), pltpu.VMEM((h, tq, LANES), _F32)],
        compiler_params=_params("parallel"),
        name="sb_sample",
    )(q, kn, vn, kp, vp, uo)


def _mix_body(x_ref, oa_ref, g1_ref, w3_ref, bg_ref, gg_ref, wm_ref, bm_ref, wb_ref, wo_ref,
              g2_ref, y1_ref, h2t_ref, *gv_ref):
    x = x_ref[...]
    tt = x.shape[0]
    h = _rms(x, g1_ref[...]).astype(_BF)
    p = _dot(h, w3_ref[...])
    u = _gelu(p[:, :WIDTH])
    vn = _rms(_gelu(p[:, WIDTH:2 * WIDTH]), gg_ref[...])
    if gv_ref:
        gv_ref[0][...] = vn
    vnb = vn.astype(_BF)
    lane = lax.broadcasted_iota(jnp.int32, (MLP_CHUNK, LANES), 1)
    bias = bm_ref[...]
    chunks = []
    for c in range(tt // MLP_CHUNK):
        cols = []
        for m in range(WIDTH // LANES):
            vc = vnb[c * MLP_CHUNK:(c + 1) * MLP_CHUNK, m * LANES:(m + 1) * LANES]
            cols.append(jnp.where(lane < GROUP_DIM_MLP, _dot(wm_ref[2 * m], vc),
                                  _dot(wm_ref[2 * m + 1], vc)))
        chunks.append(jnp.concatenate(cols, axis=1) + bias)
    mixed = chunks[0] if len(chunks) == 1 else jnp.concatenate(chunks, axis=0)
    ob = (u * mixed).astype(_BF)
    gates = jax.nn.sigmoid(p[:, 2 * WIDTH:] + bg_ref[...])
    m = gates[:, :D_MODEL] * _dot(oa_ref[...], wb_ref[0]) + gates[:, D_MODEL:] * _dot(ob, wb_ref[1])
    y1 = x + _dot(m.astype(_BF), wo_ref[...])
    y1_ref[...] = y1
    h2t_ref[...] = _rms(y1, g2_ref[...]).T.astype(_BF)


def _mix(x, oa, g1, w3, bg, gg, wm, bm, wb, wo, g2, tt, emit_gv):
    t = x.shape[0]
    row = lambda i: (i, 0)
    fixed2 = lambda i: (0, 0)
    fixed3 = lambda i: (0, 0, 0)
    out_specs = [pl.BlockSpec((tt, D_MODEL), row), pl.BlockSpec((D_MODEL, tt), lambda i: (0, i))]
    out_shape = [jax.ShapeDtypeStruct((t, D_MODEL), _F32), jax.ShapeDtypeStruct((D_MODEL, t), _BF)]
    if emit_gv:
        out_specs.append(pl.BlockSpec((tt, WIDTH), row))
        out_shape.append(jax.ShapeDtypeStruct((t, WIDTH), _F32))
    return pl.pallas_call(
        _mix_body,
        grid=(t // tt,),
        in_specs=[pl.BlockSpec((tt, D_MODEL), row),
                  pl.BlockSpec((tt, WIDTH), row),
                  pl.BlockSpec((1, D_MODEL), fixed2),
                  pl.BlockSpec((D_MODEL, 2 * WIDTH + 2 * D_MODEL), fixed2),
                  pl.BlockSpec((1, 2 * D_MODEL), fixed2),
                  pl.BlockSpec((1, WIDTH), fixed2),
                  pl.BlockSpec((N_GROUPS_MLP, MLP_CHUNK, MLP_CHUNK), fixed3),
                  pl.BlockSpec((MLP_CHUNK, WIDTH), fixed2),
                  pl.BlockSpec((2, WIDTH, D_MODEL), fixed3),
                  pl.BlockSpec((D_MODEL, D_MODEL), fixed2),
                  pl.BlockSpec((1, D_MODEL), fixed2)],
        out_specs=out_specs,
        out_shape=out_shape,
        compiler_params=_params("parallel"),
        name="mixer_out",
    )(x, oa, g1, w3, bg, gg, wm, bm, wb, wo, g2)


def _top16(xs, exact, want_rank):
    shape = xs[0].shape
    row = lax.broadcasted_iota(jnp.int32, shape, 0)
    ranks = [jnp.full(shape, N_KEYS - 1, jnp.int32) if want else None for want in want_rank]
    vals = [[] for _ in xs]
    for a in range(PEER_TOPK):
        ms = [jnp.max(x, axis=0, keepdims=True) for x in xs]
        if exact:
            idxs = [jnp.min(jnp.where(x == m, row, N_KEYS), axis=0, keepdims=True)
                    for x, m in zip(xs, ms)]
            hits = [row == idx for idx in idxs]
        else:
            hits = [x == m for x, m in zip(xs, ms)]
        xs = [jnp.where(hit, _NEG_INF, x) for hit, x in zip(hits, xs)]
        ranks = [None if rank is None else jnp.where(hit, a, rank) for hit, rank in zip(hits, ranks)]
        for v, m in zip(vals, ms):
            v.append(m)
    removed = [jnp.sum((x == _NEG_INF).astype(jnp.int32), axis=0, keepdims=True) for x in xs]
    return [jnp.concatenate(v, axis=0) for v in vals], ranks, removed


def _staircase(v1s, v2s):
    half = PEER_TOPK // 2
    upper_reach = PEER_TOPK // (half + 1) + 1
    shape = (half, v1s[0].shape[1])
    arow_lo = lax.broadcasted_iota(jnp.int32, shape, 0)
    arow_hi = arow_lo + half
    v1_lo = [v1[:half] for v1 in v1s]
    v1_hi = [v1[half:] for v1 in v1s]
    cnt_lo = [jnp.zeros(shape, jnp.int32) for _ in v1s]
    cnt_hi = [jnp.zeros(shape, jnp.int32) for _ in v1s]

    def frontier(count, v2, reach):
        nxt = jnp.full(shape, _NEG_INF, _F32)
        for b in range(reach):
            nxt = jnp.where(count == b, v2[b:b + 1, :], nxt)
        return nxt

    for _ in range(PEER_TOPK):
        c_lo = [v1 + frontier(cnt, v2, PEER_TOPK) for v1, cnt, v2 in zip(v1_lo, cnt_lo, v2s)]
        c_hi = [v1 + frontier(cnt, v2, upper_reach) for v1, cnt, v2 in zip(v1_hi, cnt_hi, v2s)]
        ms = [jnp.max(jnp.maximum(lo, hi), axis=0, keepdims=True) for lo, hi in zip(c_lo, c_hi)]
        asels = [jnp.min(jnp.minimum(jnp.where(lo == m, arow_lo, PEER_TOPK),
                                     jnp.where(hi == m, arow_hi, PEER_TOPK)), axis=0, keepdims=True)
                 for lo, hi, m in zip(c_lo, c_hi, ms)]
        cnt_lo = [cnt + (arow_lo == asel).astype(jnp.int32) for cnt, asel in zip(cnt_lo, asels)]
        cnt_hi = [cnt + (arow_hi == asel).astype(jnp.int32) for cnt, asel in zip(cnt_hi, asels)]
    return [jnp.concatenate([lo, hi], axis=0) for lo, hi in zip(cnt_lo, cnt_hi)]


def _peer_body(h2t_ref, y1_ref, wqt_ref, sk_ref, u_ref, vt_ref, gf_ref, y_ref,
               sc_ref, lr_ref, f1_ref, r2_ref, e2_ref, st_ref, wt_ref, acc_ref):
    c = pl.program_id(1)
    tt = h2t_ref.shape[1]
    nlg = tt // LANES
    ec = u_ref.shape[0]
    n1_per_chunk = ec // N_KEYS
    h2t = h2t_ref[...]

    @pl.when(c == 0)
    def _route():
        acc_ref[...] = jnp.zeros_like(acc_ref)
        kd = 2 * N_KEYS
        for h in range(PEER_HEADS):
            qt = _dot(wqt_ref[h * kd:(h + 1) * kd, :], h2t).astype(_BF)
            for half in range(2):
                st = _dot(sk_ref[h, half], qt[half * N_KEYS:(half + 1) * N_KEYS, :])
                for lg in range(nlg):
                    sc_ref[h, half, lg] = st[:, lg * LANES:(lg + 1) * LANES]

        def route_pair(idx, exact):
            slabs = [2 * idx, 2 * idx + 1]
            hs = [s // nlg for s in slabs]
            lgs = [s % nlg for s in slabs]
            s1s = [sc_ref[h, 0, lg] for h, lg in zip(hs, lgs)]
            s2s = [sc_ref[h, 1, lg] for h, lg in zip(hs, lgs)]
            vals, ranks, removed = _top16(s1s + s2s, exact, [exact, exact, True, True])
            v1s, v2s = vals[:2], vals[2:]
            stairs = _staircase(v1s, v2s)
            for i in range(2):
                h, lg, v1, v2, stair = hs[i], lgs[i], v1s[i], v2s[i], stairs[i]
                e1s = jnp.exp(v1 - v1[0:1, :])
                e2s = jnp.exp(v2 - v2[0:1, :])
                pref = jnp.zeros_like(e2s)
                for b in range(PEER_TOPK):
                    pref = pref + jnp.where(stair > b, e2s[b:b + 1, :], 0.0)
                zsum = jnp.sum(e1s * pref, axis=0, keepdims=True)
                lr = jnp.zeros(s1s[i].shape, jnp.int32)
                for a in range(PEER_TOPK):
                    is_a = (ranks[i] == a) if exact else (s1s[i] == v1[a:a + 1, :])
                    lr = jnp.where(is_a, stair[a:a + 1, :], lr)
                blocked = lr_ref.shape[2:]
                lr_ref[h, lg] = lr.astype(_F32).reshape(blocked)
                f1_ref[h, lg] = (jnp.exp(s1s[i] - v1[0:1, :]) / zsum).reshape(blocked)
                r2_ref[h, lg] = ranks[2 + i].astype(_F32).astype(_BF)
                e2_ref[h, lg] = jnp.exp(s2s[i] - v2[0:1, :]).astype(_BF)
            return removed

        def route(idx, carry):
            removed = route_pair(idx, exact=False)
            tied = sum(jnp.sum((r != PEER_TOPK).astype(jnp.int32)) for r in removed)

            @pl.when(tied > 0)
            def _redo():
                route_pair(idx, exact=True)

            return carry

        lax.fori_loop(0, (PEER_HEADS * nlg) // 2, route, 0)

    es = min(ec, PEER_SUB_EXPERTS)
    for sub in range(ec // es):
        st_ref[sub * es:(sub + 1) * es, :] = _dot(u_ref[sub * es:(sub + 1) * es, :], h2t)
    for sub in range(ec // es):
        for k in range(es // N_KEYS):
            n1_local = sub * (es // N_KEYS) + k
            r0 = sub * es + k * N_KEYS
            for lg in range(nlg):
                gate = jnp.zeros((N_KEYS, LANES), _BF)
                for h in range(PEER_HEADS):
                    lrow = lr_ref[h, lg, c, n1_local:n1_local + 1, :].astype(_BF)
                    frow = f1_ref[h, lg, c, n1_local:n1_local + 1, :].astype(_BF)
                    gate = gate + jnp.where(r2_ref[h, lg] < lrow, e2_ref[h, lg] * frow,
                                            jnp.zeros((), _BF))
                act = _gelu(st_ref[r0:r0 + N_KEYS, lg * LANES:(lg + 1) * LANES])
                wt_ref[r0:r0 + N_KEYS, lg * LANES:(lg + 1) * LANES] = gate * act.astype(_BF)
        acc_ref[...] += _dot(vt_ref[:, sub * es:(sub + 1) * es], wt_ref[sub * es:(sub + 1) * es, :])

    @pl.when(c == pl.num_programs(1) - 1)
    def _finish():
        y_ref[...] = _rms(y1_ref[...] + acc_ref[...].T, gf_ref[...])


def _peer(h2t, y1, wqt, sk, u, vt, gf, tt):
    t = y1.shape[0]
    nlg = tt // LANES
    ec = PEER_STEP_EXPERTS
    assert N_EXPERTS % ec == 0 and (PEER_HEADS * nlg) % 2 == 0
    tile = lambda i, c: (i, 0)
    once = dict(pipeline_mode=pl.Buffered(1))
    slab = lambda dt: pltpu.VMEM((PEER_HEADS, nlg, N_KEYS, LANES), dt)
    rows = lambda: pltpu.VMEM((PEER_HEADS, nlg, N_EXPERTS // ec, ec // N_KEYS, LANES), _F32)
    return pl.pallas_call(
        _peer_body,
        grid=(t // tt, N_EXPERTS // ec),
        in_specs=[pl.BlockSpec((D_MODEL, tt), lambda i, c: (0, i)),
                  pl.BlockSpec((tt, D_MODEL), tile),
                  pl.BlockSpec((PEER_HEADS * 2 * N_KEYS, D_MODEL), lambda i, c: (0, 0), **once),
                  pl.BlockSpec((PEER_HEADS, 2, N_KEYS, N_KEYS), lambda i, c: (0, 0, 0, 0), **once),
                  pl.BlockSpec((ec, D_MODEL), lambda i, c: (c, 0)),
                  pl.BlockSpec((D_MODEL, ec), lambda i, c: (0, c)),
                  pl.BlockSpec((1, D_MODEL), lambda i, c: (0, 0))],
        out_specs=pl.BlockSpec((tt, D_MODEL), tile),
        out_shape=jax.ShapeDtypeStruct((t, D_MODEL), _F32),
        scratch_shapes=[pltpu.VMEM((PEER_HEADS, 2, nlg, N_KEYS, LANES), _F32),
                        rows(), rows(), slab(_BF), slab(_BF),
                        pltpu.VMEM((ec, tt), _F32),
                        pltpu.VMEM((ec, tt), _BF),
                        pltpu.VMEM((D_MODEL, tt), _F32)],
        compiler_params=_params("parallel", "arbitrary"),
        name="peer_ffn",
    )(h2t, y1, wqt, sk, u, vt, gf)


def _to_heads(t2d, b):
    t = t2d.shape[0] // b
    return t2d.reshape(b, t, N_HEADS_SB, HEAD_DIM_SB).transpose(0, 2, 1, 3)


def _split_streams(tokens_minor, b):
    bt = tokens_minor.shape[1]
    return tokens_minor.reshape(N_HEADS_SB, HEAD_DIM_SB, b, bt // b).transpose(2, 0, 3, 1)


def kernel(x_prompt, x_sample, cache_k, cache_v, norm_mix_g, w_in, b_gate, gmlp_norm_g, w_s, b_s,
           w_branch, w_out, norm_ffn_g, w_query, sub_keys, expert_u, expert_v, norm_final_g):
    assert w_in.shape[0] == 1, "single trunk layer"
    bp, sp, _ = x_prompt.shape
    bs, ss, _ = x_sample.shape
    assert bp == 1 and sp % MLP_CHUNK == 0 and bs * ss == LANES and ss <= CHUNK

    w_in_b = w_in[0].astype(_BF)
    w_qkv = w_in_b[:, :3 * WIDTH]
    w_rest = w_in_b[:, 3 * WIDTH:]
    g_mix = norm_mix_g[0][None, :]
    g_ffn = norm_ffn_g[0][None, :]
    g_fin = norm_final_g[None, :]
    g_mlp = gmlp_norm_g[0][None, :]
    bg = b_gate[0][None, :]
    wb = w_branch[0].astype(_BF)
    wo = w_out[0].astype(_BF)
    wqt = w_query[0].T.astype(_BF)
    sk = sub_keys[0].astype(_BF)
    u_b = expert_u[0].astype(_BF)
    vt_b = expert_v[0].T.astype(_BF)
    jj = lax.broadcasted_iota(jnp.int32, (KEY_BLOCK, 2 * KEY_BLOCK), 0)
    cc = lax.broadcasted_iota(jnp.int32, (KEY_BLOCK, 2 * KEY_BLOCK), 1)
    uo = ((cc >= KEY_BLOCK) | (jj >= cc)).astype(_BF)
    uo = jnp.concatenate([uo, uo], axis=0)

    pos = jnp.arange(MLP_CHUNK)
    causal = (pos[None, :] // CHUNK) <= (pos[:, None] // CHUNK)
    wm_p = jnp.where(causal[None], w_s[0], 0.0).astype(_BF)
    bm_p = jnp.repeat(b_s[0].T, GROUP_DIM_MLP, axis=1)
    same_stream = (pos[None, :] // ss) == (pos[:, None] // ss)
    wm_s = jnp.where(same_stream[None], jnp.tile(w_s[0][:, :ss, :ss], (1, bs, bs)), 0.0).astype(_BF)
    bm_s = jnp.repeat(jnp.tile(b_s[0][:, :ss], (1, bs)).T, GROUP_DIM_MLP, axis=1)

    xp = x_prompt[0]
    q_p, k_p, v_p, kb_p, vb_p = _qkv(xp, g_mix, w_qkv, 512)
    oa_p = _sb_prompt(q_p, kb_p, vb_p, uo)
    y1_p, h2t_p = _mix(xp, oa_p, g_mix, w_rest, bg, g_mlp, wm_p, bm_p, wb, wo, g_ffn, 512, False)
    y_p = _peer(h2t_p, y1_p, wqt, sk, u_b, vt_b, g_fin, 512)

    xs = x_sample.reshape(bs * ss, D_MODEL)
    q_s, k_s, v_s, kb_s, vb_s = _qkv(xs, g_mix, w_qkv, LANES)
    keys_minor = lambda a: a.transpose(0, 1, 3, 2)
    pad = ((0, 0), (0, 0), (0, 0), (0, KEY_BLOCK - ss))
    oa_s = _sb_sample(_to_heads(q_s, bs), jnp.pad(keys_minor(_to_heads(kb_s, bs)), pad),
                      jnp.pad(keys_minor(_to_heads(vb_s, bs)), pad),
                      keys_minor(cache_k[0]), keys_minor(cache_v[0]), uo)
    oa_s = oa_s.transpose(0, 2, 1, 3).reshape(bs * ss, WIDTH).astype(_BF)
    y1_s, h2t_s, gv_s = _mix(xs, oa_s, g_mix, w_rest, bg, g_mlp, wm_s, bm_s, wb, wo, g_ffn,
                             LANES, True)
    y_s = _peer(h2t_s, y1_s, wqt, sk, u_b, vt_b, g_fin, LANES)

    return (y_p[None],
            y_s.reshape(bs, ss, D_MODEL),
            _split_streams(k_p, bp)[None],
            _split_streams(v_p, bp)[None],
            _split_streams(k_s, bs)[None],
            _split_streams(v_s, bs)[None],
            gv_s.reshape(bs, ss, WIDTH)[None])
```

```python
import functools

import jax
import jax.numpy as jnp
from jax import lax
from jax.experimental import pallas as pl
from jax.experimental.pallas import tpu as pltpu

D_MODEL = 1024
N_HEADS_SB = 8
HEAD_DIM_SB = 64
WIDTH = N_HEADS_SB * HEAD_DIM_SB
N_GROUPS_MLP = 8
GROUP_DIM_MLP = WIDTH // N_GROUPS_MLP
MLP_CHUNK = 128
CHUNK = 64
PEER_HEADS = 8
N_KEYS = 128
N_EXPERTS = N_KEYS * N_KEYS
PEER_TOPK = 16
EPS = 1e-6
SB_SCALE = HEAD_DIM_SB ** -0.5

LANES = 128
KEY_BLOCK = LANES
Q_BLOCK = LANES
SB_Q_BLOCKS_PER_STEP = 4
SB_DEAD_LOGIT = -120.0
VMEM_LIMIT_BYTES = 60 * 1024 * 1024
PEER_STEP_EXPERTS = 2048
PEER_SUB_EXPERTS = 512

_BF = jnp.bfloat16
_F32 = jnp.float32
_NEG_INF = float("-inf")


def _rms(x, g):
    return x * lax.rsqrt(jnp.mean(x * x, axis=-1, keepdims=True) + EPS) * g


def _gelu(x):
    c = 0.7978845608028654
    half = 0.5 * x
    return half + half * jnp.tanh(x * (c + (c * 0.044715) * (x * x)))


def _dot(a, b):
    return jnp.dot(a, b, preferred_element_type=_F32)


def _params(*sem):
    return pltpu.CompilerParams(dimension_semantics=sem, vmem_limit_bytes=VMEM_LIMIT_BYTES)


def _qkv_body(x_ref, g_ref, w_ref, q_ref, k_ref, v_ref, kb_ref, vb_ref):
    h = _rms(x_ref[...], g_ref[...]).astype(_BF)
    p = _dot(h, w_ref[...])
    q_ref[...] = (p[:, :WIDTH] * SB_SCALE).astype(_BF)
    k = p[:, WIDTH:2 * WIDTH]
    v = p[:, 2 * WIDTH:]
    k_ref[...] = k.T
    v_ref[...] = v.T
    kb_ref[...] = k.astype(_BF)
    vb_ref[...] = v.astype(_BF)


def _qkv(x, g, w_qkv, tt):
    t = x.shape[0]
    row = lambda i: (i, 0)
    fixed = lambda i: (0, 0)
    heads = pl.BlockSpec((WIDTH, tt), lambda i: (0, i))
    return pl.pallas_call(
        _qkv_body,
        grid=(t // tt,),
        in_specs=[pl.BlockSpec((tt, D_MODEL), row),
                  pl.BlockSpec((1, D_MODEL), fixed),
                  pl.BlockSpec((D_MODEL, 3 * WIDTH), fixed)],
        out_specs=[pl.BlockSpec((tt, WIDTH), row), heads, heads,
                   pl.BlockSpec((tt, WIDTH), row), pl.BlockSpec((tt, WIDTH), row)],
        out_shape=[jax.ShapeDtypeStruct((t, WIDTH), _BF),
                   jax.ShapeDtypeStruct((WIDTH, t), _F32),
                   jax.ShapeDtypeStruct((WIDTH, t), _F32),
                   jax.ShapeDtypeStruct((t, WIDTH), _BF),
                   jax.ShapeDtypeStruct((t, WIDTH), _BF)],
        compiler_params=_params("parallel"),
        name="qkv_proj",
    )(x, g, w_qkv)


def _sb_blocks(qs, kbs, vbs, valids, rs, uo, keys_minor=False):
    if not isinstance(valids, (list, tuple)):
        valids = [valids] * len(qs)
    contract_last = (((1,), (1,)), ((), ()))
    if keys_minor:
        zs = [_dot(q, kb) for q, kb in zip(qs, kbs)]
    else:
        zs = [lax.dot_general(q, kb, contract_last, preferred_element_type=_F32)
              for q, kb in zip(qs, kbs)]
    hls = []
    for z, valid in zip(zs, valids):
        softplus = jnp.maximum(z, 0.0) + jnp.log(1.0 + jnp.exp(-jnp.abs(z)))
        lom = jnp.where(valid, -softplus, 0.0)
        hi = lom.astype(_BF)
        lo = (lom - hi.astype(_F32)).astype(_BF)
        hls.append(jnp.concatenate([hi, lo], axis=1))
    css = [_dot(hl, uo) for hl in hls]
    ws = [jnp.where(valid, jnp.exp(z + cs[:, :KEY_BLOCK] + r), 0.0).astype(_BF)
          for z, cs, r, valid in zip(zs, css, rs, valids)]
    if keys_minor:
        pvs = [lax.dot_general(w, vb, contract_last, preferred_element_type=_F32)
               for w, vb in zip(ws, vbs)]
    else:
        pvs = [_dot(w, vb) for w, vb in zip(ws, vbs)]
    return pvs, [r + cs[:, KEY_BLOCK:] for r, cs in zip(rs, css)]


def _sb_prompt_body(q_ref, k_ref, v_ref, uo_ref, o_ref, q2_ref, acc_ref, r_ref):
    step = pl.program_id(0)
    n_pair = WIDTH // LANES
    nq = SB_Q_BLOCKS_PER_STEP
    chains = [(qb, p) for qb in range(nq) for p in range(n_pair)]
    lane = lax.broadcasted_iota(jnp.int32, (Q_BLOCK, LANES), 1)
    for ci, (qb, p) in enumerate(chains):
        q = q_ref[qb * Q_BLOCK:(qb + 1) * Q_BLOCK, p * LANES:(p + 1) * LANES]
        zero = jnp.zeros_like(q)
        q2_ref[ci] = jnp.concatenate([jnp.where(lane < HEAD_DIM_SB, q, zero),
                                      jnp.where(lane >= HEAD_DIM_SB, q, zero)], axis=0)
    acc_ref[...] = jnp.zeros_like(acc_ref)
    r_ref[...] = jnp.zeros_like(r_ref)
    row = lax.broadcasted_iota(jnp.int32, (2 * Q_BLOCK, KEY_BLOCK), 0) & (Q_BLOCK - 1)
    col = lax.broadcasted_iota(jnp.int32, (2 * Q_BLOCK, KEY_BLOCK), 1)
    last_block = step * nq + (nq - 1)

    def cond(c):
        d, rmax = c
        return jnp.logical_and(d <= last_block, rmax > SB_DEAD_LOGIT)

    def body(c):
        d, _ = c
        offs, valids = [], []
        for qb in range(nq):
            i = step * nq + qb
            j = i - d
            jc = jnp.maximum(j, 0)
            offs.append(pl.multiple_of(jc * KEY_BLOCK, KEY_BLOCK))
            valids.append(jnp.logical_and((col + jc * KEY_BLOCK) < (row + i * Q_BLOCK), j >= 0))
        pvs, rns = _sb_blocks(
            [q2_ref[ci] for ci in range(len(chains))],
            [k_ref[pl.ds(offs[qb], KEY_BLOCK), p * LANES:(p + 1) * LANES] for qb, p in chains],
            [v_ref[pl.ds(offs[qb], KEY_BLOCK), p * LANES:(p + 1) * LANES] for qb, p in chains],
            [valids[qb] for qb, _ in chains], [r_ref[ci] for ci in range(len(chains))], uo_ref[...])
        rmax = jnp.float32(_NEG_INF)
        for ci in range(len(chains)):
            acc_ref[ci] += pvs[ci]
            r_ref[ci] = rns[ci]
            rmax = jnp.maximum(rmax, jnp.max(rns[ci]))
        return d + 1, rmax

    lax.while_loop(cond, body, (jnp.int32(0), jnp.float32(0.0)))
    for ci, (qb, p) in enumerate(chains):
        acc = acc_ref[ci]
        o_ref[qb * Q_BLOCK:(qb + 1) * Q_BLOCK, p * LANES:(p + 1) * LANES] = jnp.where(
            lane < HEAD_DIM_SB, acc[:Q_BLOCK], acc[Q_BLOCK:]).astype(o_ref.dtype)


def _sb_prompt(q, kb, vb, uo):
    s = q.shape[0]
    n_chain = SB_Q_BLOCKS_PER_STEP * (WIDTH // LANES)
    rows = SB_Q_BLOCKS_PER_STEP * Q_BLOCK
    assert Q_BLOCK == KEY_BLOCK and s % rows == 0
    resident = lambda shape: pl.BlockSpec(shape, lambda i: (0, 0), pipeline_mode=pl.Buffered(1))
    return pl.pallas_call(
        _sb_prompt_body,
        grid=(s // rows,),
        in_specs=[pl.BlockSpec((rows, WIDTH), lambda i: (i, 0)),
                  resident((s, WIDTH)),
                  resident((s, WIDTH)),
                  resident((2 * KEY_BLOCK, 2 * KEY_BLOCK))],
        out_specs=pl.BlockSpec((rows, WIDTH), lambda i: (i, 0)),
        out_shape=jax.ShapeDtypeStruct((s, WIDTH), _BF),
        scratch_shapes=[pltpu.VMEM((n_chain, 2 * Q_BLOCK, LANES), _BF),
                        pltpu.VMEM((n_chain, 2 * Q_BLOCK, LANES), _F32),
                        pltpu.VMEM((n_chain, 2 * Q_BLOCK, LANES), _F32)],
        compiler_params=_params("parallel"),
        name="sb_prompt",
    )(q, kb, vb, uo)


def _sb_sample_body(q_ref, kn_ref, vn_ref, kp_ref, vp_ref, uo_ref, o_ref, acc_ref, r_ref):
    n_head, tq, _ = q_ref.shape
    heads = range(n_head)
    qs = [q_ref[hd] for hd in heads]
    row = lax.broadcasted_iota(jnp.int32, (tq, KEY_BLOCK), 0)
    col = lax.broadcasted_iota(jnp.int32, (tq, KEY_BLOCK), 1)
    pvs, r0s = _sb_blocks(qs, [kn_ref[hd] for hd in heads], [vn_ref[hd] for hd in heads], col < row,
                          [jnp.zeros((tq, LANES), _F32) for _ in heads], uo_ref[...],
                          keys_minor=True)
    rmax0 = jnp.float32(_NEG_INF)
    for hd in heads:
        acc_ref[hd] = pvs[hd]
        r_ref[hd] = r0s[hd]
        rmax0 = jnp.maximum(rmax0, jnp.max(r0s[hd]))
    always = col >= 0
    n_past = kp_ref.shape[2] // KEY_BLOCK

    def cond(c):
        j, rmax = c
        return jnp.logical_and(j >= 0, rmax > SB_DEAD_LOGIT)

    def body(c):
        j, _ = c
        off = pl.multiple_of(j * KEY_BLOCK, KEY_BLOCK)
        pvs, rns = _sb_blocks(qs, [kp_ref[hd, :, pl.ds(off, KEY_BLOCK)].astype(_BF) for hd in heads],
                              [vp_ref[hd, :, pl.ds(off, KEY_BLOCK)].astype(_BF) for hd in heads],
                              always, [r_ref[hd] for hd in heads], uo_ref[...], keys_minor=True)
        rmax = jnp.float32(_NEG_INF)
        for hd in heads:
            acc_ref[hd] += pvs[hd]
            r_ref[hd] = rns[hd]
            rmax = jnp.maximum(rmax, jnp.max(rns[hd]))
        return j - 1, rmax

    lax.while_loop(cond, body, (jnp.int32(n_past - 1), rmax0))
    o_ref[...] = acc_ref[...]


def _sb_sample(q, kn, vn, kp, vp, uo):
    b, h, tq, dh = q.shape
    past = kp.shape[3]
    blk = lambda n: pl.BlockSpec((None, h, n, dh), lambda bi: (bi, 0, 0, 0))
    blk_t = lambda n: pl.BlockSpec((None, h, dh, n), lambda bi: (bi, 0, 0, 0))
    return pl.pallas_call(
        _sb_sample_body,
        grid=(b,),
        in_specs=[blk(tq), blk_t(KEY_BLOCK), blk_t(KEY_BLOCK), blk_t(past), blk_t(past),
                  pl.BlockSpec((2 * KEY_BLOCK, 2 * KEY_BLOCK), lambda bi: (0, 0))],
        out_specs=blk(tq),
        out_shape=jax.ShapeDtypeStruct((b, h, tq, dh), _F32),
        scratch_shapes=[pltpu.VMEM((h, tq, dh), _F32), pltpu.VMEM((h, tq, LANES), _F32)],
        compiler_params=_params("parallel"),
        name="sb_sample",
    )(q, kn, vn, kp, vp, uo)


def _mix_body(x_ref, oa_ref, g1_ref, w3_ref, bg_ref, gg_ref, wm_ref, bm_ref, wb_ref, wo_ref,
              g2_ref, y1_ref, h2t_ref, *gv_ref):
    x = x_ref[...]
    tt = x.shape[0]
    h = _rms(x, g1_ref[...]).astype(_BF)
    p = _dot(h, w3_ref[...])
    u = _gelu(p[:, :WIDTH])
    vn = _rms(_gelu(p[:, WIDTH:2 * WIDTH]), gg_ref[...])
    if gv_ref:
        gv_ref[0][...] = vn
    vnb = vn.astype(_BF)
    lane = lax.broadcasted_iota(jnp.int32, (MLP_CHUNK, LANES), 1)
    bias = bm_ref[...]
    chunks = []
    for c in range(tt // MLP_CHUNK):
        cols = []
        for m in range(WIDTH // LANES):
            vc = vnb[c * MLP_CHUNK:(c + 1) * MLP_CHUNK, m * LANES:(m + 1) * LANES]
            cols.append(jnp.where(lane < GROUP_DIM_MLP, _dot(wm_ref[2 * m], vc),
                                  _dot(wm_ref[2 * m + 1], vc)))
        chunks.append(jnp.concatenate(cols, axis=1) + bias)
    mixed = chunks[0] if len(chunks) == 1 else jnp.concatenate(chunks, axis=0)
    ob = (u * mixed).astype(_BF)
    gates = jax.nn.sigmoid(p[:, 2 * WIDTH:] + bg_ref[...])
    m = gates[:, :D_MODEL] * _dot(oa_ref[...], wb_ref[0]) + gates[:, D_MODEL:] * _dot(ob, wb_ref[1])
    y1 = x + _dot(m.astype(_BF), wo_ref[...])
    y1_ref[...] = y1
    h2t_ref[...] = _rms(y1, g2_ref[...]).T.astype(_BF)


def _mix(x, oa, g1, w3, bg, gg, wm, bm, wb, wo, g2, tt, emit_gv):
    t = x.shape[0]
    row = lambda i: (i, 0)
    fixed2 = lambda i: (0, 0)
    fixed3 = lambda i: (0, 0, 0)
    out_specs = [pl.BlockSpec((tt, D_MODEL), row), pl.BlockSpec((D_MODEL, tt), lambda i: (0, i))]
    out_shape = [jax.ShapeDtypeStruct((t, D_MODEL), _F32), jax.ShapeDtypeStruct((D_MODEL, t), _BF)]
    if emit_gv:
        out_specs.append(pl.BlockSpec((tt, WIDTH), row))
        out_shape.append(jax.ShapeDtypeStruct((t, WIDTH), _F32))
    return pl.pallas_call(
        _mix_body,
        grid=(t // tt,),
        in_specs=[pl.BlockSpec((tt, D_MODEL), row),
                  pl.BlockSpec((tt, WIDTH), row),
                  pl.BlockSpec((1, D_MODEL), fixed2),
                  pl.BlockSpec((D_MODEL, 2 * WIDTH + 2 * D_MODEL), fixed2),
                  pl.BlockSpec((1, 2 * D_MODEL), fixed2),
                  pl.BlockSpec((1, WIDTH), fixed2),
                  pl.BlockSpec((N_GROUPS_MLP, MLP_CHUNK, MLP_CHUNK), fixed3),
                  pl.BlockSpec((MLP_CHUNK, WIDTH), fixed2),
                  pl.BlockSpec((2, WIDTH, D_MODEL), fixed3),
                  pl.BlockSpec((D_MODEL, D_MODEL), fixed2),
                  pl.BlockSpec((1, D_MODEL), fixed2)],
        out_specs=out_specs,
        out_shape=out_shape,
        compiler_params=_params("parallel"),
        name="mixer_out",
    )(x, oa, g1, w3, bg, gg, wm, bm, wb, wo, g2)


def _top16(xs, exact, want_rank):
    shape = xs[0].shape
    row = lax.broadcasted_iota(jnp.int32, shape, 0)
    ranks = [jnp.full(shape, N_KEYS - 1, jnp.int32) if want else None for want in want_rank]
    vals = [[] for _ in xs]
    for a in range(PEER_TOPK):
        ms = [jnp.max(x, axis=0, keepdims=True) for x in xs]
        if exact:
            idxs = [jnp.min(jnp.where(x == m, row, N_KEYS), axis=0, keepdims=True)
                    for x, m in zip(xs, ms)]
            hits = [row == idx for idx in idxs]
        else:
            hits = [x == m for x, m in zip(xs, ms)]
        xs = [jnp.where(hit, _NEG_INF, x) for hit, x in zip(hits, xs)]
        ranks = [None if rank is None else jnp.where(hit, a, rank) for hit, rank in zip(hits, ranks)]
        for v, m in zip(vals, ms):
            v.append(m)
    removed = [jnp.sum((x == _NEG_INF).astype(jnp.int32), axis=0, keepdims=True) for x in xs]
    return [jnp.concatenate(v, axis=0) for v in vals], ranks, removed


def _staircase(v1s, v2s):
    half = PEER_TOPK // 2
    upper_reach = PEER_TOPK // (half + 1) + 1
    shape = (half, v1s[0].shape[1])
    arow_lo = lax.broadcasted_iota(jnp.int32, shape, 0)
    arow_hi = arow_lo + half
    v1_lo = [v1[:half] for v1 in v1s]
    v1_hi = [v1[half:] for v1 in v1s]
    cnt_lo = [jnp.zeros(shape, jnp.int32) for _ in v1s]
    cnt_hi = [jnp.zeros(shape, jnp.int32) for _ in v1s]

    def frontier(count, v2, reach):
        nxt = jnp.full(shape, _NEG_INF, _F32)
        for b in range(reach):
            nxt = jnp.where(count == b, v2[b:b + 1, :], nxt)
        return nxt

    for _ in range(PEER_TOPK):
        c_lo = [v1 + frontier(cnt, v2, PEER_TOPK) for v1, cnt, v2 in zip(v1_lo, cnt_lo, v2s)]
        c_hi = [v1 + frontier(cnt, v2, upper_reach) for v1, cnt, v2 in zip(v1_hi, cnt_hi, v2s)]
        ms = [jnp.max(jnp.maximum(lo, hi), axis=0, keepdims=True) for lo, hi in zip(c_lo, c_hi)]
        asels = [jnp.min(jnp.minimum(jnp.where(lo == m, arow_lo, PEER_TOPK),
                                     jnp.where(hi == m, arow_hi, PEER_TOPK)), axis=0, keepdims=True)
                 for lo, hi, m in zip(c_lo, c_hi, ms)]
        cnt_lo = [cnt + (arow_lo == asel).astype(jnp.int32) for cnt, asel in zip(cnt_lo, asels)]
        cnt_hi = [cnt + (arow_hi == asel).astype(jnp.int32) for cnt, asel in zip(cnt_hi, asels)]
    return [jnp.concatenate([lo, hi], axis=0) for lo, hi in zip(cnt_lo, cnt_hi)]


def _peer_body(h2t_ref, y1_ref, wqt_ref, sk_ref, u_ref, vt_ref, gf_ref, y_ref,
               sc_ref, lr_ref, f1_ref, r2_ref, e2_ref, st_ref, wt_ref, acc_ref):
    c = pl.program_id(1)
    tt = h2t_ref.shape[1]
    nlg = tt // LANES
    ec = u_ref.shape[0]
    n1_per_chunk = ec // N_KEYS
    h2t = h2t_ref[...]

    @pl.when(c == 0)
    def _route():
        acc_ref[...] = jnp.zeros_like(acc_ref)
        kd = 2 * N_KEYS
        for h in range(PEER_HEADS):
            qt = _dot(wqt_ref[h * kd:(h + 1) * kd, :], h2t).astype(_BF)
            for half in range(2):
                st = _dot(sk_ref[h, half], qt[half * N_KEYS:(half + 1) * N_KEYS, :])
                for lg in range(nlg):
                    sc_ref[h, half, lg] = st[:, lg * LANES:(lg + 1) * LANES]

        def route_pair(idx, exact):
            slabs = [2 * idx, 2 * idx + 1]
            hs = [s // nlg for s in slabs]
            lgs = [s % nlg for s in slabs]
            s1s = [sc_ref[h, 0, lg] for h, lg in zip(hs, lgs)]
            s2s = [sc_ref[h, 1, lg] for h, lg in zip(hs, lgs)]
            vals, ranks, removed = _top16(s1s + s2s, exact, [exact, exact, True, True])
            v1s, v2s = vals[:2], vals[2:]
            stairs = _staircase(v1s, v2s)
            for i in range(2):
                h, lg, v1, v2, stair = hs[i], lgs[i], v1s[i], v2s[i], stairs[i]
                e1s = jnp.exp(v1 - v1[0:1, :])
                e2s = jnp.exp(v2 - v2[0:1, :])
                pref = jnp.zeros_like(e2s)
                for b in range(PEER_TOPK):
                    pref = pref + jnp.where(stair > b, e2s[b:b + 1, :], 0.0)
                zsum = jnp.sum(e1s * pref, axis=0, keepdims=True)
                lr = jnp.zeros(s1s[i].shape, jnp.int32)
                for a in range(PEER_TOPK):
                    is_a = (ranks[i] == a) if exact else (s1s[i] == v1[a:a + 1, :])
                    lr = jnp.where(is_a, stair[a:a + 1, :], lr)
                blocked = lr_ref.shape[2:]
                lr_ref[h, lg] = lr.astype(_F32).reshape(blocked)
                f1_ref[h, lg] = (jnp.exp(s1s[i] - v1[0:1, :]) / zsum).reshape(blocked)
                r2_ref[h, lg] = ranks[2 + i].astype(_F32).astype(_BF)
                e2_ref[h, lg] = jnp.exp(s2s[i] - v2[0:1, :]).astype(_BF)
            return removed

        def route(idx, carry):
            removed = route_pair(idx, exact=False)
            tied = sum(jnp.sum((r != PEER_TOPK).astype(jnp.int32)) for r in removed)

            @pl.when(tied > 0)
            def _redo():
                route_pair(idx, exact=True)

            return carry

        lax.fori_loop(0, (PEER_HEADS * nlg) // 2, route, 0)

    es = min(ec, PEER_SUB_EXPERTS)
    for sub in range(ec // es):
        st_ref[sub * es:(sub + 1) * es, :] = _dot(u_ref[sub * es:(sub + 1) * es, :], h2t)
    for sub in range(ec // es):
        for k in range(es // N_KEYS):
            n1_local = sub * (es // N_KEYS) + k
            r0 = sub * es + k * N_KEYS
            for lg in range(nlg):
                gate = jnp.zeros((N_KEYS, LANES), _BF)
                for h in range(PEER_HEADS):
                    lrow = lr_ref[h, lg, c, n1_local:n1_local + 1, :].astype(_BF)
                    frow = f1_ref[h, lg, c, n1_local:n1_local + 1, :].astype(_BF)
                    gate = gate + jnp.where(r2_ref[h, lg] < lrow, e2_ref[h, lg] * frow,
                                            jnp.zeros((), _BF))
                act = _gelu(st_ref[r0:r0 + N_KEYS, lg * LANES:(lg + 1) * LANES])
                wt_ref[r0:r0 + N_KEYS, lg * LANES:(lg + 1) * LANES] = gate * act.astype(_BF)
        acc_ref[...] += _dot(vt_ref[:, sub * es:(sub + 1) * es], wt_ref[sub * es:(sub + 1) * es, :])

    @pl.when(c == pl.num_programs(1) - 1)
    def _finish():
        y_ref[...] = _rms(y1_ref[...] + acc_ref[...].T, gf_ref[...])


def _peer(h2t, y1, wqt, sk, u, vt, gf, tt):
    t = y1.shape[0]
    nlg = tt // LANES
    ec = PEER_STEP_EXPERTS
    assert N_EXPERTS % ec == 0 and (PEER_HEADS * nlg) % 2 == 0
    tile = lambda i, c: (i, 0)
    once = dict(pipeline_mode=pl.Buffered(1))
    slab = lambda dt: pltpu.VMEM((PEER_HEADS, nlg, N_KEYS, LANES), dt)
    rows = lambda: pltpu.VMEM((PEER_HEADS, nlg, N_EXPERTS // ec, ec // N_KEYS, LANES), _F32)
    return pl.pallas_call(
        _peer_body,
        grid=(t // tt, N_EXPERTS // ec),
        in_specs=[pl.BlockSpec((D_MODEL, tt), lambda i, c: (0, i)),
                  pl.BlockSpec((tt, D_MODEL), tile),
                  pl.BlockSpec((PEER_HEADS * 2 * N_KEYS, D_MODEL), lambda i, c: (0, 0), **once),
                  pl.BlockSpec((PEER_HEADS, 2, N_KEYS, N_KEYS), lambda i, c: (0, 0, 0, 0), **once),
                  pl.BlockSpec((ec, D_MODEL), lambda i, c: (c, 0)),
                  pl.BlockSpec((D_MODEL, ec), lambda i, c: (0, c)),
                  pl.BlockSpec((1, D_MODEL), lambda i, c: (0, 0))],
        out_specs=pl.BlockSpec((tt, D_MODEL), tile),
        out_shape=jax.ShapeDtypeStruct((t, D_MODEL), _F32),
        scratch_shapes=[pltpu.VMEM((PEER_HEADS, 2, nlg, N_KEYS, LANES), _F32),
                        rows(), rows(), slab(_BF), slab(_BF),
                        pltpu.VMEM((ec, tt), _F32),
                        pltpu.VMEM((ec, tt), _BF),
                        pltpu.VMEM((D_MODEL, tt), _F32)],
        compiler_params=_params("parallel", "arbitrary"),
        name="peer_ffn",
    )(h2t, y1, wqt, sk, u, vt, gf)


def _to_heads(t2d, b):
    t = t2d.shape[0] // b
    return t2d.reshape(b, t, N_HEADS_SB, HEAD_DIM_SB).transpose(0, 2, 1, 3)


def _split_streams(tokens_minor, b):
    bt = tokens_minor.shape[1]
    return tokens_minor.reshape(N_HEADS_SB, HEAD_DIM_SB, b, bt // b).transpose(2, 0, 3, 1)


def kernel(x_prompt, x_sample, cache_k, cache_v, norm_mix_g, w_in, b_gate, gmlp_norm_g, w_s, b_s,
           w_branch, w_out, norm_ffn_g, w_query, sub_keys, expert_u, expert_v, norm_final_g):
    assert w_in.shape[0] == 1, "single trunk layer"
    bp, sp, _ = x_prompt.shape
    bs, ss, _ = x_sample.shape
    assert bp == 1 and sp % MLP_CHUNK == 0 and bs * ss == LANES and ss <= CHUNK

    w_in_b = w_in[0].astype(_BF)
    w_qkv = w_in_b[:, :3 * WIDTH]
    w_rest = w_in_b[:, 3 * WIDTH:]
    g_mix = norm_mix_g[0][None, :]
    g_ffn = norm_ffn_g[0][None, :]
    g_fin = norm_final_g[None, :]
    g_mlp = gmlp_norm_g[0][None, :]
    bg = b_gate[0][None, :]
    wb = w_branch[0].astype(_BF)
    wo = w_out[0].astype(_BF)
    wqt = w_query[0].T.astype(_BF)
    sk = sub_keys[0].astype(_BF)
    u_b = expert_u[0].astype(_BF)
    vt_b = expert_v[0].T.astype(_BF)
    jj = lax.broadcasted_iota(jnp.int32, (KEY_BLOCK, 2 * KEY_BLOCK), 0)
    cc = lax.broadcasted_iota(jnp.int32, (KEY_BLOCK, 2 * KEY_BLOCK), 1)
    uo = ((cc >= KEY_BLOCK) | (jj >= cc)).astype(_BF)
    uo = jnp.concatenate([uo, uo], axis=0)

    pos = jnp.arange(MLP_CHUNK)
    causal = (pos[None, :] // CHUNK) <= (pos[:, None] // CHUNK)
    wm_p = jnp.where(causal[None], w_s[0], 0.0).astype(_BF)
    bm_p = jnp.repeat(b_s[0].T, GROUP_DIM_MLP, axis=1)
    same_stream = (pos[None, :] // ss) == (pos[:, None] // ss)
    wm_s = jnp.where(same_stream[None], jnp.tile(w_s[0][:, :ss, :ss], (1, bs, bs)), 0.0).astype(_BF)
    bm_s = jnp.repeat(jnp.tile(b_s[0][:, :ss], (1, bs)).T, GROUP_DIM_MLP, axis=1)

    xp = x_prompt[0]
    q_p, k_p, v_p, kb_p, vb_p = _qkv(xp, g_mix, w_qkv, 512)
    oa_p = _sb_prompt(q_p, kb_p, vb_p, uo)
    y1_p, h2t_p = _mix(xp, oa_p, g_mix, w_rest, bg, g_mlp, wm_p, bm_p, wb, wo, g_ffn, 512, False)
    y_p = _peer(h2t_p, y1_p, wqt, sk, u_b, vt_b, g_fin, 512)

    xs = x_sample.reshape(bs * ss, D_MODEL)
    q_s, k_s, v_s, kb_s, vb_s = _qkv(xs, g_mix, w_qkv, LANES)
    keys_minor = lambda a: a.transpose(0, 1, 3, 2)
    pad = ((0, 0), (0, 0), (0, 0), (0, KEY_BLOCK - ss))
    oa_s = _sb_sample(_to_heads(q_s, bs), jnp.pad(keys_minor(_to_heads(kb_s, bs)), pad),
                      jnp.pad(keys_minor(_to_heads(vb_s, bs)), pad),
                      keys_minor(cache_k[0]), keys_minor(cache_v[0]), uo)
    oa_s = oa_s.transpose(0, 2, 1, 3).reshape(bs * ss, WIDTH).astype(_BF)
    y1_s, h2t_s, gv_s = _mix(xs, oa_s, g_mix, w_rest, bg, g_mlp, wm_s, bm_s, wb, wo, g_ffn,
                             LANES, True)
    y_s = _peer(h2t_s, y1_s, wqt, sk, u_b, vt_b, g_fin, LANES)

    return (y_p[None],
            y_s.reshape(bs, ss, D_MODEL),
            _split_streams(k_p, bp)[None],
            _split_streams(v_p, bp)[None],
            _split_streams(k_s, bs)[None],
            _split_streams(v_s, bs)[None],
            gv_s.reshape(bs, ss, WIDTH)[None])
```

```python
import functools

import jax
import jax.numpy as jnp
from jax import lax
from jax.experimental import pallas as pl
from jax.experimental.pallas import tpu as pltpu

D_MODEL = 1024
N_HEADS_SB = 8
HEAD_DIM_SB = 64
WIDTH = N_HEADS_SB * HEAD_DIM_SB
N_GROUPS_MLP = 8
GROUP_DIM_MLP = WIDTH // N_GROUPS_MLP
MLP_CHUNK = 128
CHUNK = 64
PEER_HEADS = 8
N_KEYS = 128
N_EXPERTS = N_KEYS * N_KEYS
PEER_TOPK = 16
EPS = 1e-6
SB_SCALE = HEAD_DIM_SB ** -0.5

LANES = 128
KEY_BLOCK = LANES
Q_BLOCK = LANES
SB_Q_BLOCKS_PER_STEP = 4
SB_DEAD_LOGIT = -120.0
VMEM_LIMIT_BYTES = 60 * 1024 * 1024
PEER_STEP_EXPERTS = 2048
PEER_SUB_EXPERTS = 512

_BF = jnp.bfloat16
_F32 = jnp.float32
_NEG_INF = float("-inf")


def _rms(x, g):
    return x * lax.rsqrt(jnp.mean(x * x, axis=-1, keepdims=True) + EPS) * g


def _gelu(x):
    c = 0.7978845608028654
    half = 0.5 * x
    return half + half * jnp.tanh(x * (c + (c * 0.044715) * (x * x)))


def _dot(a, b):
    return jnp.dot(a, b, preferred_element_type=_F32)


def _params(*sem):
    return pltpu.CompilerParams(dimension_semantics=sem, vmem_limit_bytes=VMEM_LIMIT_BYTES)


def _qkv_body(x_ref, g_ref, w_ref, q_ref, k_ref, v_ref, kb_ref, vb_ref):
    h = _rms(x_ref[...], g_ref[...]).astype(_BF)
    p = _dot(h, w_ref[...])
    q_ref[...] = (p[:, :WIDTH] * SB_SCALE).astype(_BF)
    k = p[:, WIDTH:2 * WIDTH]
    v = p[:, 2 * WIDTH:]
    k_ref[...] = k.T
    v_ref[...] = v.T
    kb_ref[...] = k.astype(_BF)
    vb_ref[...] = v.astype(_BF)


def _qkv(x, g, w_qkv, tt):
    t = x.shape[0]
    row = lambda i: (i, 0)
    fixed = lambda i: (0, 0)
    heads = pl.BlockSpec((WIDTH, tt), lambda i: (0, i))
    return pl.pallas_call(
        _qkv_body,
        grid=(t // tt,),
        in_specs=[pl.BlockSpec((tt, D_MODEL), row),
                  pl.BlockSpec((1, D_MODEL), fixed),
                  pl.BlockSpec((D_MODEL, 3 * WIDTH), fixed)],
        out_specs=[pl.BlockSpec((tt, WIDTH), row), heads, heads,
                   pl.BlockSpec((tt, WIDTH), row), pl.BlockSpec((tt, WIDTH), row)],
        out_shape=[jax.ShapeDtypeStruct((t, WIDTH), _BF),
                   jax.ShapeDtypeStruct((WIDTH, t), _F32),
                   jax.ShapeDtypeStruct((WIDTH, t), _F32),
                   jax.ShapeDtypeStruct((t, WIDTH), _BF),
                   jax.ShapeDtypeStruct((t, WIDTH), _BF)],
        compiler_params=_params("parallel"),
        name="qkv_proj",
    )(x, g, w_qkv)


def _sb_blocks(qs, kbs, vbs, valids, rs, uo, keys_minor=False):
    if not isinstance(valids, (list, tuple)):
        valids = [valids] * len(qs)
    contract_last = (((1,), (1,)), ((), ()))
    if keys_minor:
        zs = [_dot(q, kb) for q, kb in zip(qs, kbs)]
    else:
        zs = [lax.dot_general(q, kb, contract_last, preferred_element_type=_F32)
              for q, kb in zip(qs, kbs)]
    hls = []
    for z, valid in zip(zs, valids):
        softplus = jnp.maximum(z, 0.0) + jnp.log(1.0 + jnp.exp(-jnp.abs(z)))
        lom = jnp.where(valid, -softplus, 0.0)
        hi = lom.astype(_BF)
        lo = (lom - hi.astype(_F32)).astype(_BF)
        hls.append(jnp.concatenate([hi, lo], axis=1))
    css = [_dot(hl, uo) for hl in hls]
    ws = [jnp.where(valid, jnp.exp(z + cs[:, :KEY_BLOCK] + r), 0.0).astype(_BF)
          for z, cs, r, valid in zip(zs, css, rs, valids)]
    if keys_minor:
        pvs = [lax.dot_general(w, vb, contract_last, preferred_element_type=_F32)
               for w, vb in zip(ws, vbs)]
    else:
        pvs = [_dot(w, vb) for w, vb in zip(ws, vbs)]
    return pvs, [r + cs[:, KEY_BLOCK:] for r, cs in zip(rs, css)]


def _sb_prompt_body(q_ref, k_ref, v_ref, uo_ref, o_ref, q2_ref, acc_ref, r_ref):
    step = pl.program_id(0)
    n_pair = WIDTH // LANES
    nq = SB_Q_BLOCKS_PER_STEP
    chains = [(qb, p) for qb in range(nq) for p in range(n_pair)]
    lane = lax.broadcasted_iota(jnp.int32, (Q_BLOCK, LANES), 1)
    for ci, (qb, p) in enumerate(chains):
        q = q_ref[qb * Q_BLOCK:(qb + 1) * Q_BLOCK, p * LANES:(p + 1) * LANES]
        zero = jnp.zeros_like(q)
        q2_ref[ci] = jnp.concatenate([jnp.where(lane < HEAD_DIM_SB, q, zero),
                                      jnp.where(lane >= HEAD_DIM_SB, q, zero)], axis=0)
    acc_ref[...] = jnp.zeros_like(acc_ref)
    r_ref[...] = jnp.zeros_like(r_ref)
    row = lax.broadcasted_iota(jnp.int32, (2 * Q_BLOCK, KEY_BLOCK), 0) & (Q_BLOCK - 1)
    col = lax.broadcasted_iota(jnp.int32, (2 * Q_BLOCK, KEY_BLOCK), 1)
    last_block = step * nq + (nq - 1)

    def cond(c):
        d, rmax = c
        return jnp.logical_and(d <= last_block, rmax > SB_DEAD_LOGIT)

    def body(c):
        d, _ = c
        offs, valids = [], []
        for qb in range(nq):
            i = step * nq + qb
            j = i - d
            jc = jnp.maximum(j, 0)
            offs.append(pl.multiple_of(jc * KEY_BLOCK, KEY_BLOCK))
            valids.append(jnp.logical_and((col + jc * KEY_BLOCK) < (row + i * Q_BLOCK), j >= 0))
        pvs, rns = _sb_blocks(
            [q2_ref[ci] for ci in range(len(chains))],
            [k_ref[pl.ds(offs[qb], KEY_BLOCK), p * LANES:(p + 1) * LANES] for qb, p in chains],
            [v_ref[pl.ds(offs[qb], KEY_BLOCK), p * LANES:(p + 1) * LANES] for qb, p in chains],
            [valids[qb] for qb, _ in chains], [r_ref[ci] for ci in range(len(chains))], uo_ref[...])
        rmax = jnp.float32(_NEG_INF)
        for ci in range(len(chains)):
            acc_ref[ci] += pvs[ci]
            r_ref[ci] = rns[ci]
            rmax = jnp.maximum(rmax, jnp.max(rns[ci]))
        return d + 1, rmax

    lax.while_loop(cond, body, (jnp.int32(0), jnp.float32(0.0)))
    for ci, (qb, p) in enumerate(chains):
        acc = acc_ref[ci]
        o_ref[qb * Q_BLOCK:(qb + 1) * Q_BLOCK, p * LANES:(p + 1) * LANES] = jnp.where(
            lane < HEAD_DIM_SB, acc[:Q_BLOCK], acc[Q_BLOCK:]).astype(o_ref.dtype)


def _sb_prompt(q, kb, vb, uo):
    s = q.shape[0]
    n_chain = SB_Q_BLOCKS_PER_STEP * (WIDTH // LANES)
    rows = SB_Q_BLOCKS_PER_STEP * Q_BLOCK
    assert Q_BLOCK == KEY_BLOCK and s % rows == 0
    resident = lambda shape: pl.BlockSpec(shape, lambda i: (0, 0), pipeline_mode=pl.Buffered(1))
    return pl.pallas_call(
        _sb_prompt_body,
        grid=(s // rows,),
        in_specs=[pl.BlockSpec((rows, WIDTH), lambda i: (i, 0)),
                  resident((s, WIDTH)),
                  resident((s, WIDTH)),
                  resident((2 * KEY_BLOCK, 2 * KEY_BLOCK))],
        out_specs=pl.BlockSpec((rows, WIDTH), lambda i: (i, 0)),
        out_shape=jax.ShapeDtypeStruct((s, WIDTH), _BF),
        scratch_shapes=[pltpu.VMEM((n_chain, 2 * Q_BLOCK, LANES), _BF),
                        pltpu.VMEM((n_chain, 2 * Q_BLOCK, LANES), _F32),
                        pltpu.VMEM((n_chain, 2 * Q_BLOCK, LANES), _F32)],
        compiler_params=_params("parallel"),
        name="sb_prompt",
    )(q, kb, vb, uo)


def _sb_sample_body(q_ref, kn_ref, vn_ref, kp_ref, vp_ref, uo_ref, o_ref, acc_ref, r_ref):
    n_head, tq, _ = q_ref.shape
    heads = range(n_head)
    qs = [q_ref[hd] for hd in heads]
    row = lax.broadcasted_iota(jnp.int32, (tq, KEY_BLOCK), 0)
    col = lax.broadcasted_iota(jnp.int32, (tq, KEY_BLOCK), 1)
    pvs, r0s = _sb_blocks(qs, [kn_ref[hd] for hd in heads], [vn_ref[hd] for hd in heads], col < row,
                          [jnp.zeros((tq, LANES), _F32) for _ in heads], uo_ref[...],
                          keys_minor=True)
    rmax0 = jnp.float32(_NEG_INF)
    for hd in heads:
        acc_ref[hd] = pvs[hd]
        r_ref[hd] = r0s[hd]
        rmax0 = jnp.maximum(rmax0, jnp.max(r0s[hd]))
    always = col >= 0
    n_past = kp_ref.shape[2] // KEY_BLOCK

    def cond(c):
        j, rmax = c
        return jnp.logical_and(j >= 0, rmax > SB_DEAD_LOGIT)

    def body(c):
        j, _ = c
        off = pl.multiple_of(j * KEY_BLOCK, KEY_BLOCK)
        pvs, rns = _sb_blocks(qs, [kp_ref[hd, :, pl.ds(off, KEY_BLOCK)].astype(_BF) for hd in heads],
                              [vp_ref[hd, :, pl.ds(off, KEY_BLOCK)].astype(_BF) for hd in heads],
                              always, [r_ref[hd] for hd in heads], uo_ref[...], keys_minor=True)
        rmax = jnp.float32(_NEG_INF)
        for hd in heads:
            acc_ref[hd] += pvs[hd]
            r_ref[hd] = rns[hd]
            rmax = jnp.maximum(rmax, jnp.max(rns[hd]))
        return j - 1, rmax

    lax.while_loop(cond, body, (jnp.int32(n_past - 1), rmax0))
    o_ref[...] = acc_ref[...]


def _sb_sample(q, kn, vn, kp, vp, uo):
    b, h, tq, dh = q.shape
    past = kp.shape[3]
    blk = lambda n: pl.BlockSpec((None, h, n, dh), lambda bi: (bi, 0, 0, 0))
    blk_t = lambda n: pl.BlockSpec((None, h, dh, n), lambda bi: (bi, 0, 0, 0))
    return pl.pallas_call(
        _sb_sample_body,
        grid=(b,),
        in_specs=[blk(tq), blk_t(KEY_BLOCK), blk_t(KEY_BLOCK), blk_t(past), blk_t(past),
                  pl.BlockSpec((2 * KEY_BLOCK, 2 * KEY_BLOCK), lambda bi: (0, 0))],
        out_specs=blk(tq),
        out_shape=jax.ShapeDtypeStruct((b, h, tq, dh), _F32),
        scratch_shapes=[pltpu.VMEM((h, tq, dh), _F32), pltpu.VMEM((h, tq, LANES), _F32)],
        compiler_params=_params("parallel"),
        name="sb_sample",
    )(q, kn, vn, kp, vp, uo)


def _mix_body(x_ref, oa_ref, g1_ref, w3_ref, bg_ref, gg_ref, wm_ref, bm_ref, wb_ref, wo_ref,
              g2_ref, y1_ref, h2t_ref, *gv_ref):
    x = x_ref[...]
    tt = x.shape[0]
    h = _rms(x, g1_ref[...]).astype(_BF)
    p = _dot(h, w3_ref[...])
    u = _gelu(p[:, :WIDTH])
    vn = _rms(_gelu(p[:, WIDTH:2 * WIDTH]), gg_ref[...])
    if gv_ref:
        gv_ref[0][...] = vn
    vnb = vn.astype(_BF)
    lane = lax.broadcasted_iota(jnp.int32, (MLP_CHUNK, LANES), 1)
    bias = bm_ref[...]
    chunks = []
    for c in range(tt // MLP_CHUNK):
        cols = []
        for m in range(WIDTH // LANES):
            vc = vnb[c * MLP_CHUNK:(c + 1) * MLP_CHUNK, m * LANES:(m + 1) * LANES]
            cols.append(jnp.where(lane < GROUP_DIM_MLP, _dot(wm_ref[2 * m], vc),
                                  _dot(wm_ref[2 * m + 1], vc)))
        chunks.append(jnp.concatenate(cols, axis=1) + bias)
    mixed = chunks[0] if len(chunks) == 1 else jnp.concatenate(chunks, axis=0)
    ob = (u * mixed).astype(_BF)
    gates = jax.nn.sigmoid(p[:, 2 * WIDTH:] + bg_ref[...])
    m = gates[:, :D_MODEL] * _dot(oa_ref[...], wb_ref[0]) + gates[:, D_MODEL:] * _dot(ob, wb_ref[1])
    y1 = x + _dot(m.astype(_BF), wo_ref[...])
    y1_ref[...] = y1
    h2t_ref[...] = _rms(y1, g2_ref[...]).T.astype(_BF)


def _mix(x, oa, g1, w3, bg, gg, wm, bm, wb, wo, g2, tt, emit_gv):
    t = x.shape[0]
    row = lambda i: (i, 0)
    fixed2 = lambda i: (0, 0)
    fixed3 = lambda i: (0, 0, 0)
    out_specs = [pl.BlockSpec((tt, D_MODEL), row), pl.BlockSpec((D_MODEL, tt), lambda i: (0, i))]
    out_shape = [jax.ShapeDtypeStruct((t, D_MODEL), _F32), jax.ShapeDtypeStruct((D_MODEL, t), _BF)]
    if emit_gv:
        out_specs.append(pl.BlockSpec((tt, WIDTH), row))
        out_shape.append(jax.ShapeDtypeStruct((t, WIDTH), _F32))
    return pl.pallas_call(
        _mix_body,
        grid=(t // tt,),
        in_specs=[pl.BlockSpec((tt, D_MODEL), row),
                  pl.BlockSpec((tt, WIDTH), row),
                  pl.BlockSpec((1, D_MODEL), fixed2),
                  pl.BlockSpec((D_MODEL, 2 * WIDTH + 2 * D_MODEL), fixed2),
                  pl.BlockSpec((1, 2 * D_MODEL), fixed2),
                  pl.BlockSpec((1, WIDTH), fixed2),
                  pl.BlockSpec((N_GROUPS_MLP, MLP_CHUNK, MLP_CHUNK), fixed3),
                  pl.BlockSpec((MLP_CHUNK, WIDTH), fixed2),
                  pl.BlockSpec((2, WIDTH, D_MODEL), fixed3),
                  pl.BlockSpec((D_MODEL, D_MODEL), fixed2),
                  pl.BlockSpec((1, D_MODEL), fixed2)],
        out_specs=out_specs,
        out_shape=out_shape,
        compiler_params=_params("parallel"),
        name="mixer_out",
    )(x, oa, g1, w3, bg, gg, wm, bm, wb, wo, g2)


def _top16(xs, exact, want_rank):
    shape = xs[0].shape
    row = lax.broadcasted_iota(jnp.int32, shape, 0)
    ranks = [jnp.full(shape, N_KEYS - 1, jnp.int32) if want else None for want in want_rank]
    vals = [[] for _ in xs]
    for a in range(PEER_TOPK):
        ms = [jnp.max(x, axis=0, keepdims=True) for x in xs]
        if exact:
            idxs = [jnp.min(jnp.where(x == m, row, N_KEYS), axis=0, keepdims=True)
                    for x, m in zip(xs, ms)]
            hits = [row == idx for idx in idxs]
        else:
            hits = [x == m for x, m in zip(xs, ms)]
        xs = [jnp.where(hit, _NEG_INF, x) for hit, x in zip(hits, xs)]
        ranks = [None if rank is None else jnp.where(hit, a, rank) for hit, rank in zip(hits, ranks)]
        for v, m in zip(vals, ms):
            v.append(m)
    removed = [jnp.sum((x == _NEG_INF).astype(jnp.int32), axis=0, keepdims=True) for x in xs]
    return [jnp.concatenate(v, axis=0) for v in vals], ranks, removed


def _staircase(v1s, v2s):
    half = PEER_TOPK // 2
    upper_reach = PEER_TOPK // (half + 1) + 1
    shape = (half, v1s[0].shape[1])
    arow_lo = lax.broadcasted_iota(jnp.int32, shape, 0)
    arow_hi = arow_lo + half
    v1_lo = [v1[:half] for v1 in v1s]
    v1_hi = [v1[half:] for v1 in v1s]
    cnt_lo = [jnp.zeros(shape, jnp.int32) for _ in v1s]
    cnt_hi = [jnp.zeros(shape, jnp.int32) for _ in v1s]

    def frontier(count, v2, reach):
        nxt = jnp.full(shape, _NEG_INF, _F32)
        for b in range(reach):
            nxt = jnp.where(count == b, v2[b:b + 1, :], nxt)
        return nxt

    for _ in range(PEER_TOPK):
        c_lo = [v1 + frontier(cnt, v2, PEER_TOPK) for v1, cnt, v2 in zip(v1_lo, cnt_lo, v2s)]
        c_hi = [v1 + frontier(cnt, v2, upper_reach) for v1, cnt, v2 in zip(v1_hi, cnt_hi, v2s)]
        ms = [jnp.max(jnp.maximum(lo, hi), axis=0, keepdims=True) for lo, hi in zip(c_lo, c_hi)]
        asels = [jnp.min(jnp.minimum(jnp.where(lo == m, arow_lo, PEER_TOPK),
                                     jnp.where(hi == m, arow_hi, PEER_TOPK)), axis=0, keepdims=True)
                 for lo, hi, m in zip(c_lo, c_hi, ms)]
        cnt_lo = [cnt + (arow_lo == asel).astype(jnp.int32) for cnt, asel in zip(cnt_lo, asels)]
        cnt_hi = [cnt + (arow_hi == asel).astype(jnp.int32) for cnt, asel in zip(cnt_hi, asels)]
    return [jnp.concatenate([lo, hi], axis=0) for lo, hi in zip(cnt_lo, cnt_hi)]


def _peer_body(h2t_ref, y1_ref, wqt_ref, sk_ref, u_ref, vt_ref, gf_ref, y_ref,
               sc_ref, lr_ref, f1_ref, r2_ref, e2_ref, st_ref, wt_ref, acc_ref):
    c = pl.program_id(1)
    tt = h2t_ref.shape[1]
    nlg = tt // LANES
    ec = u_ref.shape[0]
    n1_per_chunk = ec // N_KEYS
    h2t = h2t_ref[...]

    @pl.when(c == 0)
    def _route():
        acc_ref[...] = jnp.zeros_like(acc_ref)
        kd = 2 * N_KEYS
        for h in range(PEER_HEADS):
            qt = _dot(wqt_ref[h * kd:(h + 1) * kd, :], h2t).astype(_BF)
            for half in range(2):
                st = _dot(sk_ref[h, half], qt[half * N_KEYS:(half + 1) * N_KEYS, :])
                for lg in range(nlg):
                    sc_ref[h, half, lg] = st[:, lg * LANES:(lg + 1) * LANES]

        def route_pair(idx, exact):
            slabs = [2 * idx, 2 * idx + 1]
            hs = [s // nlg for s in slabs]
            lgs = [s % nlg for s in slabs]
            s1s = [sc_ref[h, 0, lg] for h, lg in zip(hs, lgs)]
            s2s = [sc_ref[h, 1, lg] for h, lg in zip(hs, lgs)]
            vals, ranks, removed = _top16(s1s + s2s, exact, [exact, exact, True, True])
            v1s, v2s = vals[:2], vals[2:]
            stairs = _staircase(v1s, v2s)
            for i in range(2):
                h, lg, v1, v2, stair = hs[i], lgs[i], v1s[i], v2s[i], stairs[i]
                e1s = jnp.exp(v1 - v1[0:1, :])
                e2s = jnp.exp(v2 - v2[0:1, :])
                pref = jnp.zeros_like(e2s)
                for b in range(PEER_TOPK):
                    pref = pref + jnp.where(stair > b, e2s[b:b + 1, :], 0.0)
                zsum = jnp.sum(e1s * pref, axis=0, keepdims=True)
                lr = jnp.zeros(s1s[i].shape, jnp.int32)
                for a in range(PEER_TOPK):
                    is_a = (ranks[i] == a) if exact else (s1s[i] == v1[a:a + 1, :])
                    lr = jnp.where(is_a, stair[a:a + 1, :], lr)
                blocked = lr_ref.shape[2:]
                lr_ref[h, lg] = lr.astype(_F32).reshape(blocked)
                f1_ref[h, lg] = (jnp.exp(s1s[i] - v1[0:1, :]) / zsum).reshape(blocked)
                r2_ref[h, lg] = ranks[2 + i].astype(_F32).astype(_BF)
                e2_ref[h, lg] = jnp.exp(s2s[i] - v2[0:1, :]).astype(_BF)
            return removed

        def route(idx, carry):
            removed = route_pair(idx, exact=False)
            tied = sum(jnp.sum((r != PEER_TOPK).astype(jnp.int32)) for r in removed)

            @pl.when(tied > 0)
            def _redo():
                route_pair(idx, exact=True)

            return carry

        lax.fori_loop(0, (PEER_HEADS * nlg) // 2, route, 0)

    es = min(ec, PEER_SUB_EXPERTS)
    for sub in range(ec // es):
        st_ref[sub * es:(sub + 1) * es, :] = _dot(u_ref[sub * es:(sub + 1) * es, :], h2t)
    for sub in range(ec // es):
        for k in range(es // N_KEYS):
            n1_local = sub * (es // N_KEYS) + k
            r0 = sub * es + k * N_KEYS
            for lg in range(nlg):
                gate = jnp.zeros((N_KEYS, LANES), _BF)
                for h in range(PEER_HEADS):
                    tile = (16, LANES)
                    lrow = jnp.broadcast_to(lr_ref[h, lg, c, n1_local:n1_local + 1, :], tile).astype(_BF)
                    frow = jnp.broadcast_to(f1_ref[h, lg, c, n1_local:n1_local + 1, :], tile).astype(_BF)
                    lrow = jnp.tile(lrow, (N_KEYS // 16, 1))
                    frow = jnp.tile(frow, (N_KEYS // 16, 1))
                    gate = gate + jnp.where(r2_ref[h, lg] < lrow, e2_ref[h, lg] * frow,
                                            jnp.zeros((), _BF))
                act = _gelu(st_ref[r0:r0 + N_KEYS, lg * LANES:(lg + 1) * LANES])
                wt_ref[r0:r0 + N_KEYS, lg * LANES:(lg + 1) * LANES] = gate * act.astype(_BF)
        acc_ref[...] += _dot(vt_ref[:, sub * es:(sub + 1) * es], wt_ref[sub * es:(sub + 1) * es, :])

    @pl.when(c == pl.num_programs(1) - 1)
    def _finish():
        y_ref[...] = _rms(y1_ref[...] + acc_ref[...].T, gf_ref[...])


def _peer(h2t, y1, wqt, sk, u, vt, gf, tt):
    t = y1.shape[0]
    nlg = tt // LANES
    ec = PEER_STEP_EXPERTS
    assert N_EXPERTS % ec == 0 and (PEER_HEADS * nlg) % 2 == 0
    tile = lambda i, c: (i, 0)
    once = dict(pipeline_mode=pl.Buffered(1))
    slab = lambda dt: pltpu.VMEM((PEER_HEADS, nlg, N_KEYS, LANES), dt)
    rows = lambda: pltpu.VMEM((PEER_HEADS, nlg, N_EXPERTS // ec, ec // N_KEYS, LANES), _F32)
    return pl.pallas_call(
        _peer_body,
        grid=(t // tt, N_EXPERTS // ec),
        in_specs=[pl.BlockSpec((D_MODEL, tt), lambda i, c: (0, i)),
                  pl.BlockSpec((tt, D_MODEL), tile),
                  pl.BlockSpec((PEER_HEADS * 2 * N_KEYS, D_MODEL), lambda i, c: (0, 0), **once),
                  pl.BlockSpec((PEER_HEADS, 2, N_KEYS, N_KEYS), lambda i, c: (0, 0, 0, 0), **once),
                  pl.BlockSpec((ec, D_MODEL), lambda i, c: (c, 0)),
                  pl.BlockSpec((D_MODEL, ec), lambda i, c: (0, c)),
                  pl.BlockSpec((1, D_MODEL), lambda i, c: (0, 0))],
        out_specs=pl.BlockSpec((tt, D_MODEL), tile),
        out_shape=jax.ShapeDtypeStruct((t, D_MODEL), _F32),
        scratch_shapes=[pltpu.VMEM((PEER_HEADS, 2, nlg, N_KEYS, LANES), _F32),
                        rows(), rows(), slab(_BF), slab(_BF),
                        pltpu.VMEM((ec, tt), _F32),
                        pltpu.VMEM((ec, tt), _BF),
                        pltpu.VMEM((D_MODEL, tt), _F32)],
        compiler_params=_params("parallel", "arbitrary"),
        name="peer_ffn",
    )(h2t, y1, wqt, sk, u, vt, gf)


def _to_heads(t2d, b):
    t = t2d.shape[0] // b
    return t2d.reshape(b, t, N_HEADS_SB, HEAD_DIM_SB).transpose(0, 2, 1, 3)


def _split_streams(tokens_minor, b):
    bt = tokens_minor.shape[1]
    return tokens_minor.reshape(N_HEADS_SB, HEAD_DIM_SB, b, bt // b).transpose(2, 0, 3, 1)


def kernel(x_prompt, x_sample, cache_k, cache_v, norm_mix_g, w_in, b_gate, gmlp_norm_g, w_s, b_s,
           w_branch, w_out, norm_ffn_g, w_query, sub_keys, expert_u, expert_v, norm_final_g):
    assert w_in.shape[0] == 1, "single trunk layer"
    bp, sp, _ = x_prompt.shape
    bs, ss, _ = x_sample.shape
    assert bp == 1 and sp % MLP_CHUNK == 0 and bs * ss == LANES and ss <= CHUNK

    w_in_b = w_in[0].astype(_BF)
    w_qkv = w_in_b[:, :3 * WIDTH]
    w_rest = w_in_b[:, 3 * WIDTH:]
    g_mix = norm_mix_g[0][None, :]
    g_ffn = norm_ffn_g[0][None, :]
    g_fin = norm_final_g[None, :]
    g_mlp = gmlp_norm_g[0][None, :]
    bg = b_gate[0][None, :]
    wb = w_branch[0].astype(_BF)
    wo = w_out[0].astype(_BF)
    wqt = w_query[0].T.astype(_BF)
    sk = sub_keys[0].astype(_BF)
    u_b = expert_u[0].astype(_BF)
    vt_b = expert_v[0].T.astype(_BF)
    jj = lax.broadcasted_iota(jnp.int32, (KEY_BLOCK, 2 * KEY_BLOCK), 0)
    cc = lax.broadcasted_iota(jnp.int32, (KEY_BLOCK, 2 * KEY_BLOCK), 1)
    uo = ((cc >= KEY_BLOCK) | (jj >= cc)).astype(_BF)
    uo = jnp.concatenate([uo, uo], axis=0)

    pos = jnp.arange(MLP_CHUNK)
    causal = (pos[None, :] // CHUNK) <= (pos[:, None] // CHUNK)
    wm_p = jnp.where(causal[None], w_s[0], 0.0).astype(_BF)
    bm_p = jnp.repeat(b_s[0].T, GROUP_DIM_MLP, axis=1)
    same_stream = (pos[None, :] // ss) == (pos[:, None] // ss)
    wm_s = jnp.where(same_stream[None], jnp.tile(w_s[0][:, :ss, :ss], (1, bs, bs)), 0.0).astype(_BF)
    bm_s = jnp.repeat(jnp.tile(b_s[0][:, :ss], (1, bs)).T, GROUP_DIM_MLP, axis=1)

    xp = x_prompt[0]
    q_p, k_p, v_p, kb_p, vb_p = _qkv(xp, g_mix, w_qkv, 512)
    oa_p = _sb_prompt(q_p, kb_p, vb_p, uo)
    y1_p, h2t_p = _mix(xp, oa_p, g_mix, w_rest, bg, g_mlp, wm_p, bm_p, wb, wo, g_ffn, 512, False)
    y_p = _peer(h2t_p, y1_p, wqt, sk, u_b, vt_b, g_fin, 512)

    xs = x_sample.reshape(bs * ss, D_MODEL)
    q_s, k_s, v_s, kb_s, vb_s = _qkv(xs, g_mix, w_qkv, LANES)
    keys_minor = lambda a: a.transpose(0, 1, 3, 2)
    pad = ((0, 0), (0, 0), (0, 0), (0, KEY_BLOCK - ss))
    oa_s = _sb_sample(_to_heads(q_s, bs), jnp.pad(keys_minor(_to_heads(kb_s, bs)), pad),
                      jnp.pad(keys_minor(_to_heads(vb_s, bs)), pad),
                      keys_minor(cache_k[0]), keys_minor(cache_v[0]), uo)
    oa_s = oa_s.transpose(0, 2, 1, 3).reshape(bs * ss, WIDTH).astype(_BF)
    y1_s, h2t_s, gv_s = _mix(xs, oa_s, g_mix, w_rest, bg, g_mlp, wm_s, bm_s, wb, wo, g_ffn,
                             LANES, True)
    y_s = _peer(h2t_s, y1_s, wqt, sk, u_b, vt_b, g_fin, LANES)

    return (y_p[None],
            y_s.reshape(bs, ss, D_MODEL),
            _split_streams(k_p, bp)[None],
            _split_streams(v_p, bp)[None],
            _split_streams(k_s, bs)[None],
            _split_streams(v_s, bs)[None],
            gv_s.reshape(bs, ss, WIDTH)[None])
```
